```python
import math
import jax, jax.numpy as jnp
from jax import lax
import numpy as np

D_MODEL = 1024
BATCH = 8
SEQ = 2048
DEPTH = 4
DEC_BATCH = 32
DEC_SEQ = 2048
PAST_LEN = 128

N_MIXERS = 3
DIFF_HEAD_DIM = 64
DIFF_HEADS = D_MODEL // (2 * DIFF_HEAD_DIM)
ROPE_THETA = 10000.0
Q_BLOCK = 128
CONV_WIDTH = 31
CONV_PAD = (CONV_WIDTH - 1) // 2
CHUNK = 128
SG_GROUPS = 8
SG_GROUP_DIM = D_MODEL // SG_GROUPS
D_FF = ((8 * D_MODEL // 3 + 255) // 256) * 256
NORM_EPS = 1e-6
LN_EPS = 1e-5

kernel_name = "hybrid_diffattn_conformer_sgmlp_encoder"


def rmsnorm(x, g, eps=NORM_EPS):
    xf = x.astype(jnp.float32)
    y = xf * lax.rsqrt(jnp.mean(xf * xf, axis=-1, keepdims=True) + eps)
    return (y * g.astype(jnp.float32)).astype(x.dtype)


def layernorm(x, g, b, eps=LN_EPS):
    xf = x.astype(jnp.float32)
    mu = jnp.mean(xf, axis=-1, keepdims=True)
    var = jnp.mean(jnp.square(xf - mu), axis=-1, keepdims=True)
    y = (xf - mu) * lax.rsqrt(var + eps)
    return (y * g.astype(jnp.float32) + b.astype(jnp.float32)).astype(x.dtype)


def rope_tables(seq_len, dim):
    pos = jnp.arange(seq_len, dtype=jnp.float32)
    inv_freq = 1.0 / (ROPE_THETA ** (jnp.arange(0, dim, 2, dtype=jnp.float32) / dim))
    ang = pos[:, None] * inv_freq[None, :]
    return jnp.cos(ang), jnp.sin(ang)


def apply_rope(x, cos, sin):
    x1, x2 = jnp.split(x.astype(jnp.float32), 2, axis=-1)
    c = cos[None, :, None, None, :]
    s = sin[None, :, None, None, :]
    return jnp.concatenate([x1 * c - x2 * s, x1 * s + x2 * c], axis=-1).astype(x.dtype)


def lambda_init_fn(layer_idx):
    return 0.8 - 0.6 * math.exp(-0.3 * layer_idx)


def diff_attention(h, w_qkv, lam_q1, lam_k1, lam_q2, lam_k2, g_subln, w_o, lambda_init):
    b, s, _ = h.shape
    q, k, v = jnp.split(h @ w_qkv, 3, axis=-1)
    q = q.reshape(b, s, DIFF_HEADS, 2, DIFF_HEAD_DIM)
    k = k.reshape(b, s, DIFF_HEADS, 2, DIFF_HEAD_DIM)
    v = v.reshape(b, s, DIFF_HEADS, 2 * DIFF_HEAD_DIM)
    cos, sin = rope_tables(s, DIFF_HEAD_DIM)
    q = apply_rope(q, cos, sin) * (DIFF_HEAD_DIM ** -0.5)
    k = apply_rope(k, cos, sin)
    lam = (jnp.exp(jnp.sum(lam_q1.astype(jnp.float32) * lam_k1.astype(jnp.float32)))
           - jnp.exp(jnp.sum(lam_q2.astype(jnp.float32) * lam_k2.astype(jnp.float32)))
           + lambda_init)
    nb = s // Q_BLOCK
    qb = q.reshape(b, nb, Q_BLOCK, DIFF_HEADS, 2, DIFF_HEAD_DIM).transpose(1, 0, 2, 3, 4, 5)

    def block(q_blk):
        sc = jnp.einsum('bqhcd,bkhcd->bhcqk', q_blk, k).astype(jnp.float32)
        p = jax.nn.softmax(sc, axis=-1)
        a = p[:, :, 0] - lam * p[:, :, 1]
        return jnp.einsum('bhqk,bkhe->bqhe', a.astype(v.dtype), v)

    o = lax.map(block, qb)
    o = o.transpose(1, 0, 2, 3, 4).reshape(b, s, DIFF_HEADS, 2 * DIFF_HEAD_DIM)
    o = rmsnorm(o, g_subln, eps=LN_EPS) * (1.0 - lambda_init)
    return o.reshape(b, s, D_MODEL) @ w_o


def conformer_conv(h, w_pw1, b_pw1, w_dw, b_dw, g_cln, b_cln, w_pw2, b_pw2):
    a, g = jnp.split(h @ w_pw1 + b_pw1, 2, axis=-1)
    z = a * jax.nn.sigmoid(g)
    z = lax.conv_general_dilated(
        z, w_dw[:, None, :], window_strides=(1,), padding=[(CONV_PAD, CONV_PAD)],
        dimension_numbers=('NWC', 'WIO', 'NWC'), feature_group_count=D_MODEL) + b_dw
    z = jax.nn.silu(layernorm(z, g_cln, b_cln))
    return z @ w_pw2 + b_pw2


def spatial_gating(h, w_uv, b_uv, g_sln, b_sln, w_s, b_s, w_o, b_o):
    b, s, _ = h.shape
    u, v = jnp.split(jax.nn.gelu(h @ w_uv + b_uv), 2, axis=-1)
    v = layernorm(v, g_sln, b_sln)
    vc = v.reshape(b, s // CHUNK, CHUNK, SG_GROUPS, SG_GROUP_DIM)
    mixed = jnp.einsum('gpq,bcqgd->bcpgd', w_s, vc) + b_s.T[:, :, None]
    return (u * mixed.reshape(b, s, D_MODEL)) @ w_o + b_o


def swiglu(h, w_gate_up, w_down):
    g, u = jnp.split(h @ w_gate_up, 2, axis=-1)
    return (jax.nn.silu(g) * u) @ w_down


def setup_inputs(seed: int = 0) -> dict:
    key = jax.random.key(seed)
    keys = iter(jax.random.split(key, 512))

    def nrm(shape, scale):
        return scale * jax.random.normal(next(keys), shape, jnp.float32)

    def gain(shape):
        return 1.0 + nrm(shape, 0.02)

    D = D_MODEL
    p = {}
    p["x_prompt"] = nrm((BATCH, SEQ, D), 1.0)
    p["x_sample"] = nrm((DEC_BATCH, DEC_SEQ, D), 1.0)
    for i in range(DEPTH):
        n = f"l{i}_"
        p[n + "g_mix_pre"] = gain((D,))
        p[n + "g_mix_post"] = gain((D,))
        kind = i % N_MIXERS
        if kind == 0:
            p[n + "w_qkv"] = nrm((D, 3 * D), D ** -0.5)
            p[n + "lam_q1"] = nrm((DIFF_HEAD_DIM,), 0.1)
            p[n + "lam_k1"] = nrm((DIFF_HEAD_DIM,), 0.1)
            p[n + "lam_q2"] = nrm((DIFF_HEAD_DIM,), 0.1)
            p[n + "lam_k2"] = nrm((DIFF_HEAD_DIM,), 0.1)
            p[n + "g_subln"] = gain((2 * DIFF_HEAD_DIM,))
            p[n + "w_o"] = nrm((D, D), D ** -0.5)
        elif kind == 1:
            p[n + "w_pw1"] = nrm((D, 2 * D), D ** -0.5)
            p[n + "b_pw1"] = nrm((2 * D,), 0.02)
            p[n + "w_dw"] = nrm((CONV_WIDTH, D), CONV_WIDTH ** -0.5)
            p[n + "b_dw"] = nrm((D,), 0.02)
            p[n + "g_cln"] = gain((D,))
            p[n + "b_cln"] = nrm((D,), 0.02)
            p[n + "w_pw2"] = nrm((D, D), D ** -0.5)
            p[n + "b_pw2"] = nrm((D,), 0.02)
        else:
            p[n + "w_uv"] = nrm((D, 2 * D), D ** -0.5)
            p[n + "b_uv"] = nrm((2 * D,), 0.02)
            p[n + "g_sln"] = gain((D,))
            p[n + "b_sln"] = nrm((D,), 0.02)
            p[n + "w_s"] = nrm((SG_GROUPS, CHUNK, CHUNK), CHUNK ** -0.5)
            p[n + "b_s"] = gain((SG_GROUPS, CHUNK))
            p[n + "w_o"] = nrm((D, D), D ** -0.5)
            p[n + "b_o"] = nrm((D,), 0.02)
        p[n + "g_ffn_pre"] = gain((D,))
        p[n + "g_ffn_post"] = gain((D,))
        p[n + "w_gate_up"] = nrm((D, 2 * D_FF), D ** -0.5)
        p[n + "w_down"] = nrm((D_FF, D), D_FF ** -0.5)
    return p


def reference(x_prompt, x_sample,
              l0_g_mix_pre, l0_g_mix_post, l0_w_qkv, l0_lam_q1, l0_lam_k1, l0_lam_q2, l0_lam_k2, l0_g_subln, l0_w_o,
              l0_g_ffn_pre, l0_g_ffn_post, l0_w_gate_up, l0_w_down,
              l1_g_mix_pre, l1_g_mix_post, l1_w_pw1, l1_b_pw1, l1_w_dw, l1_b_dw, l1_g_cln, l1_b_cln, l1_w_pw2, l1_b_pw2,
              l1_g_ffn_pre, l1_g_ffn_post, l1_w_gate_up, l1_w_down,
              l2_g_mix_pre, l2_g_mix_post, l2_w_uv, l2_b_uv, l2_g_sln, l2_b_sln, l2_w_s, l2_b_s, l2_w_o, l2_b_o,
              l2_g_ffn_pre, l2_g_ffn_post, l2_w_gate_up, l2_w_down,
              l3_g_mix_pre, l3_g_mix_post, l3_w_qkv, l3_lam_q1, l3_lam_k1, l3_lam_q2, l3_lam_k2, l3_g_subln, l3_w_o,
              l3_g_ffn_pre, l3_g_ffn_post, l3_w_gate_up, l3_w_down):
    mixer_params = [
        (l0_w_qkv, l0_lam_q1, l0_lam_k1, l0_lam_q2, l0_lam_k2, l0_g_subln, l0_w_o),
        (l1_w_pw1, l1_b_pw1, l1_w_dw, l1_b_dw, l1_g_cln, l1_b_cln, l1_w_pw2, l1_b_pw2),
        (l2_w_uv, l2_b_uv, l2_g_sln, l2_b_sln, l2_w_s, l2_b_s, l2_w_o, l2_b_o),
        (l3_w_qkv, l3_lam_q1, l3_lam_k1, l3_lam_q2, l3_lam_k2, l3_g_subln, l3_w_o),
    ]
    norm_params = [
        (l0_g_mix_pre, l0_g_mix_post, l0_g_ffn_pre, l0_g_ffn_post),
        (l1_g_mix_pre, l1_g_mix_post, l1_g_ffn_pre, l1_g_ffn_post),
        (l2_g_mix_pre, l2_g_mix_post, l2_g_ffn_pre, l2_g_ffn_post),
        (l3_g_mix_pre, l3_g_mix_post, l3_g_ffn_pre, l3_g_ffn_post),
    ]
    ffn_params = [
        (l0_w_gate_up, l0_w_down),
        (l1_w_gate_up, l1_w_down),
        (l2_w_gate_up, l2_w_down),
        (l3_w_gate_up, l3_w_down),
    ]

    def trunk(x):
        for i in range(DEPTH):
            g_mix_pre, g_mix_post, g_ffn_pre, g_ffn_post = norm_params[i]
            h = rmsnorm(x, g_mix_pre)
            kind = i % N_MIXERS
            if kind == 0:
                m = diff_attention(h, *mixer_params[i], lambda_init=lambda_init_fn(i))
            elif kind == 1:
                m = conformer_conv(h, *mixer_params[i])
            else:
                m = spatial_gating(h, *mixer_params[i])
            x = x + rmsnorm(m, g_mix_post)
            f = swiglu(rmsnorm(x, g_ffn_pre), *ffn_params[i])
            x = x + rmsnorm(f, g_ffn_post)
        return x

    y_prompt = trunk(x_prompt)
    y_sample = trunk(x_sample)
    return (y_prompt, y_sample)
```

```python
import functools
import math

import jax
import jax.numpy as jnp
from jax import lax
from jax.experimental import pallas as pl
from jax.experimental.pallas import tpu as pltpu

F32 = jnp.float32
BF16 = jnp.bfloat16

N_MIXERS = 3
DIFF_HEAD_DIM = 64
ROPE_THETA = 10000.0
CONV_WIDTH = 31
CONV_PAD = (CONV_WIDTH - 1) // 2
CHUNK = 128
SG_GROUPS = 8
NORM_EPS = 1e-6
LN_EPS = 1e-5

LANES = 128
SUBLANES = 8
HALO = 16
VMEM_LIMIT = 56 * 1024 * 1024

TOKEN_TILE = 512
Q_TILE = 512
FF_CHUNK = 1408


def _lambda_init(layer_idx):
    return 0.8 - 0.6 * math.exp(-0.3 * layer_idx)


def _rms(x, g, eps):
    return x * lax.rsqrt(jnp.mean(x * x, axis=-1, keepdims=True) + eps) * g


def _layernorm(x, g, b, eps):
    mu = jnp.mean(x, axis=-1, keepdims=True)
    xc = x - mu
    var = jnp.mean(xc * xc, axis=-1, keepdims=True)
    return xc * lax.rsqrt(var + eps) * g + b


def _sigmoid(x):
    return 1.0 / (1.0 + jnp.exp(-x))


def _dot(a, b):
    return jnp.dot(a, b, preferred_element_type=F32)


def _const_spec(shape):
    nd = len(shape)
    return pl.BlockSpec(shape, lambda *_: (0,) * nd, pipeline_mode=pl.Buffered(1))


def _params(n_grid_dims):
    return pltpu.CompilerParams(
        dimension_semantics=("parallel",) * n_grid_dims,
        vmem_limit_bytes=VMEM_LIMIT,
    )


def _ffn_kernel(x_ref, gpre_ref, gpost_ref, wgu_ref, wd_ref, o_ref, *, d_ff, ff_chunk):
    x = x_ref[...]
    h = _rms(x, gpre_ref[...], NORM_EPS).astype(BF16)
    acc = None
    for c in range(d_ff // ff_chunk):
        lo = c * ff_chunk
        g = _dot(h, wgu_ref[:, lo:lo + ff_chunk])
        u = _dot(h, wgu_ref[:, d_ff + lo:d_ff + lo + ff_chunk])
        a = (g * _sigmoid(g) * u).astype(BF16)
        part = _dot(a, wd_ref[lo:lo + ff_chunk, :])
        acc = part if acc is None else acc + part
    o_ref[...] = x + _rms(acc, gpost_ref[...], NORM_EPS)


def _ffn(x, g_pre, g_post, w_gate_up, w_down):
    t, d = x.shape
    d_ff = w_down.shape[0]
    tm = TOKEN_TILE
    row = pl.BlockSpec((tm, d), lambda i: (i, 0))
    return pl.pallas_call(
        functools.partial(_ffn_kernel, d_ff=d_ff, ff_chunk=FF_CHUNK),
        grid=(t // tm,),
        in_specs=[row, _const_spec((1, d)), _const_spec((1, d)),
                  _const_spec((d, 2 * d_ff)), _const_spec((d_ff, d))],
        out_specs=row,
        out_shape=jax.ShapeDtypeStruct((t, d), F32),
        compiler_params=_params(1),
        name="ffn",
    )(x, g_pre, g_post, w_gate_up, w_down)


def _rope_block(xb, cos, sin_signed, first_half):
    x_up = pltpu.roll(xb, LANES - DIFF_HEAD_DIM // 2, 1)
    x_dn = pltpu.roll(xb, DIFF_HEAD_DIM // 2, 1)
    return xb * cos + jnp.where(first_half, x_up, x_dn) * sin_signed


def _qkv_kernel(x_ref, g_ref, w_ref, cos_ref, sin_ref, q_ref, k_ref, v_ref):
    d = x_ref.shape[1]
    h = _rms(x_ref[...], g_ref[...], NORM_EPS).astype(BF16)
    cos = cos_ref[...]
    sin_signed = sin_ref[...]
    lane = lax.broadcasted_iota(jnp.int32, (1, LANES), 1)
    first_half = (lane % DIFF_HEAD_DIM) < (DIFF_HEAD_DIM // 2)
    q_scale = DIFF_HEAD_DIM ** -0.5
    for hb in range(d // LANES):
        lo = hb * LANES
        qb = _dot(h, w_ref[:, lo:lo + LANES])
        q_ref[:, lo:lo + LANES] = (_rope_block(qb, cos, sin_signed, first_half) * q_scale).astype(BF16)
        kb = _dot(h, w_ref[:, d + lo:d + lo + LANES])
        k_ref[:, lo:lo + LANES] = _rope_block(kb, cos, sin_signed, first_half).astype(BF16)
    v_ref[...] = _dot(h, w_ref[:, 2 * d:]).astype(BF16)


def _qkv(x, g_pre, w_qkv, cos_tab, sin_tab, seq):
    t, d = x.shape
    tm = TOKEN_TILE
    row = pl.BlockSpec((tm, d), lambda i: (i, 0))
    tiles_per_seq = seq // tm
    tab = pl.BlockSpec((tm, LANES), lambda i: (i % tiles_per_seq, 0))
    out = jax.ShapeDtypeStruct((t, d), BF16)
    return pl.pallas_call(
        _qkv_kernel,
        grid=(t // tm,),
        in_specs=[row, _const_spec((1, d)), _const_spec((d, 3 * d)), tab, tab],
        out_specs=[row, row, row],
        out_shape=[out, out, out],
        compiler_params=_params(1),
        name="qkv_rope",
    )(x, g_pre, w_qkv, cos_tab, sin_tab)


def _attn_kernel(lq1_ref, lk1_ref, lq2_ref, lk2_ref, gsub_ref, q_ref, k_ref, v_ref, o_ref, *, lambda_init):
    lam = (jnp.exp(jnp.sum(lq1_ref[...] * lk1_ref[...], axis=-1, keepdims=True))
           - jnp.exp(jnp.sum(lq2_ref[...] * lk2_ref[...], axis=-1, keepdims=True))
           + lambda_init)
    q = q_ref[...]
    k = k_ref[...]
    v = v_ref[...]
    lane = lax.broadcasted_iota(jnp.int32, (1, LANES), 1)
    zero = jnp.zeros_like(q)

    def softmax_pv(qm):
        s = lax.dot_general(qm, k, (((1,), (1,)), ((), ())), preferred_element_type=F32)
        e = jnp.exp(s - jnp.max(s, axis=-1, keepdims=True))
        l = jnp.sum(e, axis=-1, keepdims=True)
        return _dot(e.astype(BF16), v) / l

    o1 = softmax_pv(jnp.where(lane < DIFF_HEAD_DIM, q, zero))
    o2 = softmax_pv(jnp.where(lane >= DIFF_HEAD_DIM, q, zero))
    o = o1 - lam * o2
    o = _rms(o, gsub_ref[...], LN_EPS) * (1.0 - lambda_init)
    o_ref[...] = o.astype(BF16)


def _attention(q, k, v, lam_q1, lam_k1, lam_q2, lam_k2, g_subln, lambda_init):
    b, s, d = q.shape
    heads = d // LANES
    tq = Q_TILE
    q_spec = pl.BlockSpec((None, tq, LANES), lambda bi, hi, qi: (bi, qi, hi))
    kv_spec = pl.BlockSpec((None, s, LANES), lambda bi, hi, qi: (bi, 0, hi))
    return pl.pallas_call(
        functools.partial(_attn_kernel, lambda_init=lambda_init),
        grid=(b, heads, s // tq),
        in_specs=[_const_spec((1, DIFF_HEAD_DIM))] * 4 + [_const_spec((1, LANES)), q_spec, kv_spec, kv_spec],
        out_specs=q_spec,
        out_shape=jax.ShapeDtypeStruct((b, s, d), BF16),
        compiler_params=_params(3),
        name="diff_attn",
    )(lam_q1, lam_k1, lam_q2, lam_k2, g_subln, q, k, v)


def _proj_kernel(a_ref, x_ref, w_ref, gpost_ref, o_ref):
    m = _dot(a_ref[...], w_ref[...])
    o_ref[...] = x_ref[...] + _rms(m, gpost_ref[...], NORM_EPS)


def _proj_residual(a, x, w, g_post):
    t, d = x.shape
    tm = TOKEN_TILE
    row = pl.BlockSpec((tm, d), lambda i: (i, 0))
    return pl.pallas_call(
        _proj_kernel,
        grid=(t // tm,),
        in_specs=[row, row, _const_spec((d, d)), _const_spec((1, d))],
        out_specs=row,
        out_shape=jax.ShapeDtypeStruct((t, d), F32),
        compiler_params=_params(1),
        name="attn_out_proj",
    )(a, x, w, g_post)


def _conv_kernel(xp_ref, x_ref, xn_ref, gpre_ref, w1_ref, b1_ref, wdw_ref, bdw_ref, gcln_ref, bcln_ref,
                 w2_ref, b2_ref, gpost_ref, o_ref, zbuf, *, tiles_per_seq):
    ts, d = x_ref.shape
    i = pl.program_id(0)
    at_seq_start = (i % tiles_per_seq) == 0
    at_seq_end = (i % tiles_per_seq) == tiles_per_seq - 1

    def glu(xv):
        h = _rms(xv, gpre_ref[...], NORM_EPS).astype(BF16)
        a = _dot(h, w1_ref[:, :d]) + b1_ref[:, :d]
        g = _dot(h, w1_ref[:, d:]) + b1_ref[:, d:]
        return a * _sigmoid(g)

    x = x_ref[...]
    zbuf[HALO:HALO + ts, :] = glu(x)
    zbuf[0:HALO, :] = jnp.where(at_seq_start, 0.0, glu(xp_ref[...]))
    zbuf[HALO + ts:, :] = jnp.where(at_seq_end, 0.0, glu(xn_ref[...]))

    acc = jnp.zeros((ts, d), F32) + bdw_ref[...]
    for j in range(CONV_WIDTH):
        off = HALO - CONV_PAD + j
        acc = acc + wdw_ref[j:j + 1, :] * zbuf[off:off + ts, :]

    y = _layernorm(acc, gcln_ref[...], bcln_ref[...], LN_EPS)
    y = (y * _sigmoid(y)).astype(BF16)
    m = _dot(y, w2_ref[...]) + b2_ref[...]
    o_ref[...] = x + _rms(m, gpost_ref[...], NORM_EPS)


def _conv_mixer(x, seq, g_pre, w_pw1, b_pw1, w_dw, b_dw, g_cln, b_cln, w_pw2, b_pw2, g_post):
    t, d = x.shape
    ts = TOKEN_TILE
    halo_per_tile = ts // HALO
    n_halo_blocks = t // HALO
    row = pl.BlockSpec((ts, d), lambda i: (i, 0))
    prev = pl.BlockSpec((HALO, d), lambda i: (jnp.maximum(i * halo_per_tile - 1, 0), 0))
    nxt = pl.BlockSpec((HALO, d), lambda i: (jnp.minimum((i + 1) * halo_per_tile, n_halo_blocks - 1), 0))
    return pl.pallas_call(
        functools.partial(_conv_kernel, tiles_per_seq=seq // ts),
        grid=(t // ts,),
        in_specs=[prev, row, nxt, _const_spec((1, d)), _const_spec((d, 2 * d)), _const_spec((1, 2 * d)),
                  _const_spec((CONV_WIDTH, d)), _const_spec((1, d)), _const_spec((1, d)), _const_spec((1, d)),
                  _const_spec((d, d)), _const_spec((1, d)), _const_spec((1, d))],
        out_specs=row,
        out_shape=jax.ShapeDtypeStruct((t, d), F32),
        scratch_shapes=[pltpu.VMEM((ts + 2 * HALO, d), F32)],
        compiler_params=_params(1),
        name="conformer_conv",
    )(x, x, x, g_pre, w_pw1, b_pw1, w_dw, b_dw, g_cln, b_cln, w_pw2, b_pw2, g_post)


def _sg_kernel(x_ref, gpre_ref, wuv_ref, buv_ref, gsln_ref, bsln_ref, ws_ref, bs_ref, wo_ref, bo_ref,
               gpost_ref, o_ref, gated):
    tm, d = x_ref.shape
    x = x_ref[...]
    h = _rms(x, gpre_ref[...], NORM_EPS).astype(BF16)
    u = jax.nn.gelu(_dot(h, wuv_ref[:, :d]) + buv_ref[:, :d])
    v = jax.nn.gelu(_dot(h, wuv_ref[:, d:]) + buv_ref[:, d:])
    v = _layernorm(v, gsln_ref[...], bsln_ref[...], LN_EPS).astype(BF16)
    for g in range(SG_GROUPS):
        cl = g * LANES
        w_g = ws_ref[g]
        b_g = bs_ref[g]
        for c in range(tm // CHUNK):
            rl = c * CHUNK
            mixed = _dot(w_g, v[rl:rl + CHUNK, cl:cl + LANES]) + b_g
            gated[rl:rl + CHUNK, cl:cl + LANES] = (u[rl:rl + CHUNK, cl:cl + LANES] * mixed).astype(BF16)
    m = _dot(gated[...], wo_ref[...]) + bo_ref[...]
    o_ref[...] = x + _rms(m, gpost_ref[...], NORM_EPS)


def _sg_mixer(x, g_pre, w_uv, b_uv, g_sln, b_sln, w_s, b_s_bcast, w_o, b_o, g_post):
    t, d = x.shape
    tm = TOKEN_TILE
    row = pl.BlockSpec((tm, d), lambda i: (i, 0))
    return pl.pallas_call(
        _sg_kernel,
        grid=(t // tm,),
        in_specs=[row, _const_spec((1, d)), _const_spec((d, 2 * d)), _const_spec((1, 2 * d)),
                  _const_spec((1, d)), _const_spec((1, d)), _const_spec((SG_GROUPS, CHUNK, CHUNK)),
                  _const_spec((SG_GROUPS, CHUNK, LANES)), _const_spec((d, d)), _const_spec((1, d)),
                  _const_spec((1, d))],
        out_specs=row,
        out_shape=jax.ShapeDtypeStruct((t, d), F32),
        scratch_shapes=[pltpu.VMEM((tm, d), BF16)],
        compiler_params=_params(1),
        name="spatial_gating",
    )(x, g_pre, w_uv, b_uv, g_sln, b_sln, w_s, b_s_bcast, w_o, b_o, g_post)


def _rope_tables(seq):
    pos = jnp.arange(seq, dtype=F32)
    inv_freq = 1.0 / (ROPE_THETA ** (jnp.arange(0, DIFF_HEAD_DIM, 2, dtype=F32) / DIFF_HEAD_DIM))
    ang = pos[:, None] * inv_freq[None, :]
    cos, sin = jnp.cos(ang), jnp.sin(ang)
    reps = LANES // DIFF_HEAD_DIM
    cos_tab = jnp.tile(jnp.concatenate([cos, cos], axis=-1), (1, reps))
    sin_tab = jnp.tile(jnp.concatenate([-sin, sin], axis=-1), (1, reps))
    return cos_tab, sin_tab


def _row(v):
    return v.reshape(1, -1)


def kernel(x_prompt, x_sample, l0_g_mix_pre, l0_g_mix_post, l0_w_qkv, l0_lam_q1, l0_lam_k1, l0_lam_q2, l0_lam_k2, l0_g_subln, l0_w_o, l0_g_ffn_pre, l0_g_ffn_post, l0_w_gate_up, l0_w_down, l1_g_mix_pre, l1_g_mix_post, l1_w_pw1, l1_b_pw1, l1_w_dw, l1_b_dw, l1_g_cln, l1_b_cln, l1_w_pw2, l1_b_pw2, l1_g_ffn_pre, l1_g_ffn_post, l1_w_gate_up, l1_w_down, l2_g_mix_pre, l2_g_mix_post, l2_w_uv, l2_b_uv, l2_g_sln, l2_b_sln, l2_w_s, l2_b_s, l2_w_o, l2_b_o, l2_g_ffn_pre, l2_g_ffn_post, l2_w_gate_up, l2_w_down, l3_g_mix_pre, l3_g_mix_post, l3_w_qkv, l3_lam_q1, l3_lam_k1, l3_lam_q2, l3_lam_k2, l3_g_subln, l3_w_o, l3_g_ffn_pre, l3_g_ffn_post, l3_w_gate_up, l3_w_down):
    p = dict(locals())
    depth = 4
    seq = x_prompt.shape[1]
    cos_tab, sin_tab = _rope_tables(seq)

    def w16(name):
        return p[name].astype(BF16)

    layers = []
    for i in range(depth):
        n = f"l{i}_"
        kind = i % N_MIXERS
        lp = {"kind": kind,
              "g_mix_pre": _row(p[n + "g_mix_pre"]), "g_mix_post": _row(p[n + "g_mix_post"]),
              "g_ffn_pre": _row(p[n + "g_ffn_pre"]), "g_ffn_post": _row(p[n + "g_ffn_post"]),
              "w_gate_up": w16(n + "w_gate_up"), "w_down": w16(n + "w_down")}
        if kind == 0:
            lp.update(w_qkv=w16(n + "w_qkv"), w_o=w16(n + "w_o"), g_subln=_row(p[n + "g_subln"]),
                      lam=[_row(p[n + k]) for k in ("lam_q1", "lam_k1", "lam_q2", "lam_k2")],
                      lambda_init=_lambda_init(i))
        elif kind == 1:
            lp.update(w_pw1=w16(n + "w_pw1"), b_pw1=_row(p[n + "b_pw1"]), w_dw=p[n + "w_dw"],
                      b_dw=_row(p[n + "b_dw"]), g_cln=_row(p[n + "g_cln"]), b_cln=_row(p[n + "b_cln"]),
                      w_pw2=w16(n + "w_pw2"), b_pw2=_row(p[n + "b_pw2"]))
        else:
            b_s = p[n + "b_s"]
            lp.update(w_uv=w16(n + "w_uv"), b_uv=_row(p[n + "b_uv"]), g_sln=_row(p[n + "g_sln"]),
                      b_sln=_row(p[n + "b_sln"]), w_s=w16(n + "w_s"),
                      b_s=jnp.broadcast_to(b_s[:, :, None], b_s.shape + (LANES,)),
                      w_o=w16(n + "w_o"), b_o=_row(p[n + "b_o"]))
        layers.append(lp)

    def trunk(x3):
        b, s, d = x3.shape
        x = x3.reshape(b * s, d)
        for lp in layers:
            if lp["kind"] == 0:
                q, k, v = _qkv(x, lp["g_mix_pre"], lp["w_qkv"], cos_tab, sin_tab, s)
                a = _attention(q.reshape(b, s, d), k.reshape(b, s, d), v.reshape(b, s, d),
                               *lp["lam"], lp["g_subln"], lp["lambda_init"])
                x = _proj_residual(a.reshape(b * s, d), x, lp["w_o"], lp["g_mix_post"])
            elif lp["kind"] == 1:
                x = _conv_mixer(x, s, lp["g_mix_pre"], lp["w_pw1"], lp["b_pw1"], lp["w_dw"], lp["b_dw"],
                                lp["g_cln"], lp["b_cln"], lp["w_pw2"], lp["b_pw2"], lp["g_mix_post"])
            else:
                x = _sg_mixer(x, lp["g_mix_pre"], lp["w_uv"], lp["b_uv"], lp["g_sln"], lp["b_sln"],
                              lp["w_s"], lp["b_s"], lp["w_o"], lp["b_o"], lp["g_mix_post"])
            x = _ffn(x, lp["g_ffn_pre"], lp["g_ffn_post"], lp["w_gate_up"], lp["w_down"])
        return x.reshape(b, s, d)

    return (trunk(x_prompt), trunk(x_sample))
```

```python
import functools
import math

import jax
import jax.numpy as jnp
from jax import lax
from jax.experimental import pallas as pl
from jax.experimental.pallas import tpu as pltpu

F32 = jnp.float32
BF16 = jnp.bfloat16

N_MIXERS = 3
DIFF_HEAD_DIM = 64
ROPE_THETA = 10000.0
CONV_WIDTH = 31
CONV_PAD = (CONV_WIDTH - 1) // 2
CHUNK = 128
SG_GROUPS = 8
NORM_EPS = 1e-6
LN_EPS = 1e-5
LOG2_E = 1.4426950408889634

LANES = 128
SUBLANES = 8
HALO = 16
VMEM_LIMIT = 56 * 1024 * 1024

TOKEN_TILE = 512
Q_TILE = 256
FF_CHUNK = 256


def _lambda_init(layer_idx):
    return 0.8 - 0.6 * math.exp(-0.3 * layer_idx)


def _rms(x, g, eps):
    return x * lax.rsqrt(jnp.mean(x * x, axis=-1, keepdims=True) + eps) * g


def _layernorm(x, g, b, eps):
    mu = jnp.mean(x, axis=-1, keepdims=True)
    xc = x - mu
    var = jnp.mean(xc * xc, axis=-1, keepdims=True)
    return xc * lax.rsqrt(var + eps) * g + b


def _sigmoid(x):
    return 1.0 / (1.0 + jnp.exp(-x))


def _dot(a, b):
    return jnp.dot(a, b, preferred_element_type=F32)


def _const_spec(shape):
    nd = len(shape)
    return pl.BlockSpec(shape, lambda *_: (0,) * nd, pipeline_mode=pl.Buffered(1))


def _params(n_grid_dims):
    return pltpu.CompilerParams(
        dimension_semantics=("parallel",) * n_grid_dims,
        vmem_limit_bytes=VMEM_LIMIT,
    )


def _ffn_kernel(*refs, d_ff, ff_chunk, fuse_mixer_proj):
    if fuse_mixer_proj:
        a_ref, wo_ref, gmix_ref, x_ref, gpre_ref, gpost_ref, wgu_ref, wd_ref, o_ref = refs
        x = x_ref[...] + _rms(_dot(a_ref[...], wo_ref[...]), gmix_ref[...], NORM_EPS)
    else:
        x_ref, gpre_ref, gpost_ref, wgu_ref, wd_ref, o_ref = refs
        x = x_ref[...]
    h = _rms(x, gpre_ref[...], NORM_EPS).astype(BF16)
    acc = None
    for c in range(d_ff // ff_chunk):
        lo = c * ff_chunk
        g = _dot(h, wgu_ref[:, lo:lo + ff_chunk])
        u = _dot(h, wgu_ref[:, d_ff + lo:d_ff + lo + ff_chunk])
        a = (g * _sigmoid(g) * u).astype(BF16)
        part = _dot(a, wd_ref[lo:lo + ff_chunk, :])
        acc = part if acc is None else acc + part
    o_ref[...] = x + _rms(acc, gpost_ref[...], NORM_EPS)


def _ffn(x, g_pre, g_post, w_gate_up, w_down, mixer_proj=None):
    t, d = x.shape
    d_ff = w_down.shape[0]
    tm = TOKEN_TILE
    row = pl.BlockSpec((tm, d), lambda i: (i, 0))
    in_specs = [row, _const_spec((1, d)), _const_spec((1, d)), _const_spec((d, 2 * d_ff)), _const_spec((d_ff, d))]
    args = [x, g_pre, g_post, w_gate_up, w_down]
    if mixer_proj is not None:
        in_specs = [row, _const_spec((d, d)), _const_spec((1, d))] + in_specs
        args = list(mixer_proj) + args
    return pl.pallas_call(
        functools.partial(_ffn_kernel, d_ff=d_ff, ff_chunk=FF_CHUNK, fuse_mixer_proj=mixer_proj is not None),
        grid=(t // tm,),
        in_specs=in_specs,
        out_specs=row,
        out_shape=jax.ShapeDtypeStruct((t, d), F32),
        compiler_params=_params(1),
        name="ffn",
    )(*args)


def _rope_block(xb, cos, sin_signed, first_half):
    x_up = pltpu.roll(xb, LANES - DIFF_HEAD_DIM // 2, 1)
    x_dn = pltpu.roll(xb, DIFF_HEAD_DIM // 2, 1)
    return xb * cos + jnp.where(first_half, x_up, x_dn) * sin_signed


def _qkv_kernel(x_ref, g_ref, w_ref, cos_ref, sin_ref, q_ref, k_ref, v_ref):
    d = x_ref.shape[1]
    h = _rms(x_ref[...], g_ref[...], NORM_EPS).astype(BF16)
    cos = cos_ref[...]
    sin_signed = sin_ref[...]
    lane = lax.broadcasted_iota(jnp.int32, (1, LANES), 1)
    first_half = (lane % DIFF_HEAD_DIM) < (DIFF_HEAD_DIM // 2)
    q_scale = DIFF_HEAD_DIM ** -0.5 * LOG2_E
    q = _dot(h, w_ref[:, :d])
    for hb in range(d // LANES):
        lo = hb * LANES
        q_ref[:, lo:lo + LANES] = (_rope_block(q[:, lo:lo + LANES], cos, sin_signed, first_half) * q_scale).astype(BF16)
    k = _dot(h, w_ref[:, d:2 * d])
    for hb in range(d // LANES):
        lo = hb * LANES
        k_ref[:, lo:lo + LANES] = _rope_block(k[:, lo:lo + LANES], cos, sin_signed, first_half).astype(BF16)
    v_ref[...] = _dot(h, w_ref[:, 2 * d:]).astype(BF16)


def _qkv(x, g_pre, w_qkv, cos_tab, sin_tab, seq):
    t, d = x.shape
    tm = TOKEN_TILE
    row = pl.BlockSpec((tm, d), lambda i: (i, 0))
    tiles_per_seq = seq // tm
    tab = pl.BlockSpec((tm, LANES), lambda i: (i % tiles_per_seq, 0))
    out = jax.ShapeDtypeStruct((t, d), BF16)
    return pl.pallas_call(
        _qkv_kernel,
        grid=(t // tm,),
        in_specs=[row, _const_spec((1, d)), _const_spec((d, 3 * d)), tab, tab],
        out_specs=[row, row, row],
        out_shape=[out, out, out],
        compiler_params=_params(1),
        name="qkv_rope",
    )(x, g_pre, w_qkv, cos_tab, sin_tab)


def _attn_kernel(lq1_ref, lk1_ref, lq2_ref, lk2_ref, gsub_ref, q_ref, k_ref, v_ref, o_ref, v_ones, *,
                 lambda_init, q_tile):
    lam = (jnp.exp(jnp.sum(lq1_ref[...] * lk1_ref[...], axis=-1, keepdims=True))
           - jnp.exp(jnp.sum(lq2_ref[...] * lk2_ref[...], axis=-1, keepdims=True))
           + lambda_init)
    k = k_ref[...]
    v_ones[:, :LANES] = v_ref[...]
    v_ones[:, LANES:] = jnp.ones(v_ref.shape, BF16)
    v1 = v_ones[...]
    lane = lax.broadcasted_iota(jnp.int32, (1, LANES), 1)

    def softmax_pv(qm):
        s = lax.dot_general(qm, k, (((1,), (1,)), ((), ())), preferred_element_type=F32)
        e = jnp.exp2(s - jnp.max(s, axis=-1, keepdims=True))
        ol = _dot(e.astype(BF16), v1)
        return ol[:, :LANES] / ol[:, LANES:LANES + 1]

    for t in range(q_ref.shape[0] // q_tile):
        q = q_ref[t * q_tile:(t + 1) * q_tile, :]
        zero = jnp.zeros_like(q)
        o1 = softmax_pv(jnp.where(lane < DIFF_HEAD_DIM, q, zero))
        o2 = softmax_pv(jnp.where(lane >= DIFF_HEAD_DIM, q, zero))
        o = o1 - lam * o2
        o = _rms(o, gsub_ref[...], LN_EPS) * (1.0 - lambda_init)
        o_ref[t * q_tile:(t + 1) * q_tile, :] = o.astype(BF16)


def _attention(q, k, v, lam_q1, lam_k1, lam_q2, lam_k2, g_subln, lambda_init):
    b, s, d = q.shape
    heads = d // LANES
    seq_spec = pl.BlockSpec((None, s, LANES), lambda bi, hi: (bi, 0, hi))
    return pl.pallas_call(
        functools.partial(_attn_kernel, lambda_init=lambda_init, q_tile=Q_TILE),
        grid=(b, heads),
        in_specs=[_const_spec((1, DIFF_HEAD_DIM))] * 4 + [_const_spec((1, LANES)), seq_spec, seq_spec, seq_spec],
        out_specs=seq_spec,
        out_shape=jax.ShapeDtypeStruct((b, s, d), BF16),
        scratch_shapes=[pltpu.VMEM((s, 2 * LANES), BF16)],
        compiler_params=_params(2),
        name="diff_attn",
    )(lam_q1, lam_k1, lam_q2, lam_k2, g_subln, q, k, v)


def _conv_kernel(xp_ref, x_ref, xn_ref, gpre_ref, w1_ref, b1_ref, wdw_ref, bdw_ref, gcln_ref, bcln_ref,
                 w2_ref, b2_ref, gpost_ref, o_ref, zbuf, *, tiles_per_seq):
    ts, d = x_ref.shape
    i = pl.program_id(0)
    at_seq_start = (i % tiles_per_seq) == 0
    at_seq_end = (i % tiles_per_seq) == tiles_per_seq - 1

    def glu(xv):
        h = _rms(xv, gpre_ref[...], NORM_EPS).astype(BF16)
        a = _dot(h, w1_ref[:, :d]) + b1_ref[:, :d]
        g = _dot(h, w1_ref[:, d:]) + b1_ref[:, d:]
        return a * _sigmoid(g)

    x = x_ref[...]
    zbuf[HALO:HALO + ts, :] = glu(x)
    zbuf[0:HALO, :] = jnp.where(at_seq_start, 0.0, glu(xp_ref[...]))
    zbuf[HALO + ts:, :] = jnp.where(at_seq_end, 0.0, glu(xn_ref[...]))

    acc = None
    for phase in range(SUBLANES):
        part = None
        for j in range(CONV_WIDTH):
            off = HALO - CONV_PAD + j
            if off % SUBLANES != phase:
                continue
            base = off - phase
            term = wdw_ref[j:j + 1, :] * zbuf[base:base + ts + SUBLANES, :]
            part = term if part is None else part + term
        if part is None:
            continue
        part = part[phase:phase + ts, :]
        acc = part if acc is None else acc + part
    acc = acc + bdw_ref[...]

    y = _layernorm(acc, gcln_ref[...], bcln_ref[...], LN_EPS)
    y = (y * _sigmoid(y)).astype(BF16)
    m = _dot(y, w2_ref[...]) + b2_ref[...]
    o_ref[...] = x + _rms(m, gpost_ref[...], NORM_EPS)


def _conv_mixer(x, seq, g_pre, w_pw1, b_pw1, w_dw, b_dw, g_cln, b_cln, w_pw2, b_pw2, g_post):
    t, d = x.shape
    ts = TOKEN_TILE
    halo_per_tile = ts // HALO
    n_halo_blocks = t // HALO
    row = pl.BlockSpec((ts, d), lambda i: (i, 0))
    prev = pl.BlockSpec((HALO, d), lambda i: (jnp.maximum(i * halo_per_tile - 1, 0), 0))
    nxt = pl.BlockSpec((HALO, d), lambda i: (jnp.minimum((i + 1) * halo_per_tile, n_halo_blocks - 1), 0))
    return pl.pallas_call(
        functools.partial(_conv_kernel, tiles_per_seq=seq // ts),
        grid=(t // ts,),
        in_specs=[prev, row, nxt, _const_spec((1, d)), _const_spec((d, 2 * d)), _const_spec((1, 2 * d)),
                  _const_spec((CONV_WIDTH, d)), _const_spec((1, d)), _const_spec((1, d)), _const_spec((1, d)),
                  _const_spec((d, d)), _const_spec((1, d)), _const_spec((1, d))],
        out_specs=row,
        out_shape=jax.ShapeDtypeStruct((t, d), F32),
        scratch_shapes=[pltpu.VMEM((ts + 2 * HALO, d), F32)],
        compiler_params=_params(1),
        name="conformer_conv",
    )(x, x, x, g_pre, w_pw1, b_pw1, w_dw, b_dw, g_cln, b_cln, w_pw2, b_pw2, g_post)


def _sg_kernel(x_ref, gpre_ref, wuv_ref, buv_ref, gsln_ref, bsln_ref, ws_ref, bs_ref, wo_ref, bo_ref,
               gpost_ref, o_ref, gated):
    tm, d = x_ref.shape
    x = x_ref[...]
    h = _rms(x, gpre_ref[...], NORM_EPS).astype(BF16)
    u = jax.nn.gelu(_dot(h, wuv_ref[:, :d]) + buv_ref[:, :d])
    v = jax.nn.gelu(_dot(h, wuv_ref[:, d:]) + buv_ref[:, d:])
    v = _layernorm(v, gsln_ref[...], bsln_ref[...], LN_EPS).astype(BF16)
    for g in range(SG_GROUPS):
        cl = g * LANES
        w_g = ws_ref[g]
        b_g = bs_ref[g]
        for c in range(tm // CHUNK):
            rl = c * CHUNK
            mixed = _dot(w_g, v[rl:rl + CHUNK, cl:cl + LANES]) + b_g
            gated[rl:rl + CHUNK, cl:cl + LANES] = (u[rl:rl + CHUNK, cl:cl + LANES] * mixed).astype(BF16)
    m = _dot(gated[...], wo_ref[...]) + bo_ref[...]
    o_ref[...] = x + _rms(m, gpost_ref[...], NORM_EPS)


def _sg_mixer(x, g_pre, w_uv, b_uv, g_sln, b_sln, w_s, b_s_bcast, w_o, b_o, g_post):
    t, d = x.shape
    tm = TOKEN_TILE
    row = pl.BlockSpec((tm, d), lambda i: (i, 0))
    return pl.pallas_call(
        _sg_kernel,
        grid=(t // tm,),
        in_specs=[row, _const_spec((1, d)), _const_spec((d, 2 * d)), _const_spec((1, 2 * d)),
                  _const_spec((1, d)), _const_spec((1, d)), _const_spec((SG_GROUPS, CHUNK, CHUNK)),
                  _const_spec((SG_GROUPS, CHUNK, LANES)), _const_spec((d, d)), _const_spec((1, d)),
                  _const_spec((1, d))],
        out_specs=row,
        out_shape=jax.ShapeDtypeStruct((t, d), F32),
        scratch_shapes=[pltpu.VMEM((tm, d), BF16)],
        compiler_params=_params(1),
        name="spatial_gating",
    )(x, g_pre, w_uv, b_uv, g_sln, b_sln, w_s, b_s_bcast, w_o, b_o, g_post)


def _rope_tables(seq):
    pos = jnp.arange(seq, dtype=F32)
    inv_freq = 1.0 / (ROPE_THETA ** (jnp.arange(0, DIFF_HEAD_DIM, 2, dtype=F32) / DIFF_HEAD_DIM))
    ang = pos[:, None] * inv_freq[None, :]
    cos, sin = jnp.cos(ang), jnp.sin(ang)
    reps = LANES // DIFF_HEAD_DIM
    cos_tab = jnp.tile(jnp.concatenate([cos, cos], axis=-1), (1, reps))
    sin_tab = jnp.tile(jnp.concatenate([-sin, sin], axis=-1), (1, reps))
    return cos_tab, sin_tab


def _row(v):
    return v.reshape(1, -1)


def kernel(x_prompt, x_sample, l0_g_mix_pre, l0_g_mix_post, l0_w_qkv, l0_lam_q1, l0_lam_k1, l0_lam_q2, l0_lam_k2, l0_g_subln, l0_w_o, l0_g_ffn_pre, l0_g_ffn_post, l0_w_gate_up, l0_w_down, l1_g_mix_pre, l1_g_mix_post, l1_w_pw1, l1_b_pw1, l1_w_dw, l1_b_dw, l1_g_cln, l1_b_cln, l1_w_pw2, l1_b_pw2, l1_g_ffn_pre, l1_g_ffn_post, l1_w_gate_up, l1_w_down, l2_g_mix_pre, l2_g_mix_post, l2_w_uv, l2_b_uv, l2_g_sln, l2_b_sln, l2_w_s, l2_b_s, l2_w_o, l2_b_o, l2_g_ffn_pre, l2_g_ffn_post, l2_w_gate_up, l2_w_down, l3_g_mix_pre, l3_g_mix_post, l3_w_qkv, l3_lam_q1, l3_lam_k1, l3_lam_q2, l3_lam_k2, l3_g_subln, l3_w_o, l3_g_ffn_pre, l3_g_ffn_post, l3_w_gate_up, l3_w_down):
    p = dict(locals())
    depth = 4
    seq = x_prompt.shape[1]
    cos_tab, sin_tab = _rope_tables(seq)

    def w16(name):
        return p[name].astype(BF16)

    layers = []
    for i in range(depth):
        n = f"l{i}_"
        kind = i % N_MIXERS
        lp = {"kind": kind,
              "g_mix_pre": _row(p[n + "g_mix_pre"]), "g_mix_post": _row(p[n + "g_mix_post"]),
              "g_ffn_pre": _row(p[n + "g_ffn_pre"]), "g_ffn_post": _row(p[n + "g_ffn_post"]),
              "w_gate_up": w16(n + "w_gate_up"), "w_down": w16(n + "w_down")}
        if kind == 0:
            lp.update(w_qkv=w16(n + "w_qkv"), w_o=w16(n + "w_o"), g_subln=_row(p[n + "g_subln"]),
                      lam=[_row(p[n + k]) for k in ("lam_q1", "lam_k1", "lam_q2", "lam_k2")],
                      lambda_init=_lambda_init(i))
        elif kind == 1:
            lp.update(w_pw1=w16(n + "w_pw1"), b_pw1=_row(p[n + "b_pw1"]), w_dw=p[n + "w_dw"],
                      b_dw=_row(p[n + "b_dw"]), g_cln=_row(p[n + "g_cln"]), b_cln=_row(p[n + "b_cln"]),
                      w_pw2=w16(n + "w_pw2"), b_pw2=_row(p[n + "b_pw2"]))
        else:
            b_s = p[n + "b_s"]
            lp.update(w_uv=w16(n + "w_uv"), b_uv=_row(p[n + "b_uv"]), g_sln=_row(p[n + "g_sln"]),
                      b_sln=_row(p[n + "b_sln"]), w_s=w16(n + "w_s"),
                      b_s=jnp.broadcast_to(b_s[:, :, None], b_s.shape + (LANES,)),
                      w_o=w16(n + "w_o"), b_o=_row(p[n + "b_o"]))
        layers.append(lp)

    def trunk(x3):
        b, s, d = x3.shape
        x = x3.reshape(b * s, d)
        for lp in layers:
            mixer_proj = None
            if lp["kind"] == 0:
                q, k, v = _qkv(x, lp["g_mix_pre"], lp["w_qkv"], cos_tab, sin_tab, s)
                a = _attention(q.reshape(b, s, d), k.reshape(b, s, d), v.reshape(b, s, d),
                               *lp["lam"], lp["g_subln"], lp["lambda_init"])
                mixer_proj = (a.reshape(b * s, d), lp["w_o"], lp["g_mix_post"])
            elif lp["kind"] == 1:
                x = _conv_mixer(x, s, lp["g_mix_pre"], lp["w_pw1"], lp["b_pw1"], lp["w_dw"], lp["b_dw"],
                                lp["g_cln"], lp["b_cln"], lp["w_pw2"], lp["b_pw2"], lp["g_mix_post"])
            else:
                x = _sg_mixer(x, lp["g_mix_pre"], lp["w_uv"], lp["b_uv"], lp["g_sln"], lp["b_sln"],
                              lp["w_s"], lp["b_s"], lp["w_o"], lp["b_o"], lp["g_mix_post"])
            x = _ffn(x, lp["g_ffn_pre"], lp["g_ffn_post"], lp["w_gate_up"], lp["w_down"], mixer_proj)
        return x.reshape(b, s, d)

    return (trunk(x_prompt), trunk(x_sample))
```

```python
import functools
import math

import jax
import jax.numpy as jnp
from jax import lax
from jax.experimental import pallas as pl
from jax.experimental.pallas import tpu as pltpu

F32 = jnp.float32
BF16 = jnp.bfloat16

N_MIXERS = 3
DIFF_HEAD_DIM = 64
ROPE_THETA = 10000.0
CONV_WIDTH = 31
CONV_PAD = (CONV_WIDTH - 1) // 2
CHUNK = 128
SG_GROUPS = 8
NORM_EPS = 1e-6
LN_EPS = 1e-5
LOG2_E = 1.4426950408889634

LANES = 128
SUBLANES = 8
HALO = 16
VMEM_LIMIT = 56 * 1024 * 1024

TOKEN_TILE = 512
Q_TILE = 256
FF_CHUNK = 256


def _lambda_init(layer_idx):
    return 0.8 - 0.6 * math.exp(-0.3 * layer_idx)


def _rms(x, g, eps):
    return x * lax.rsqrt(jnp.mean(x * x, axis=-1, keepdims=True) + eps) * g


def _layernorm(x, g, b, eps):
    mu = jnp.mean(x, axis=-1, keepdims=True)
    xc = x - mu
    var = jnp.mean(xc * xc, axis=-1, keepdims=True)
    return xc * lax.rsqrt(var + eps) * g + b


def _sigmoid(x):
    return 1.0 / (1.0 + jnp.exp(-x))


def _dot(a, b):
    return jnp.dot(a, b, preferred_element_type=F32)


def _const_spec(shape):
    nd = len(shape)
    return pl.BlockSpec(shape, lambda *_: (0,) * nd, pipeline_mode=pl.Buffered(1))


def _params(dimension_semantics):
    return pltpu.CompilerParams(dimension_semantics=dimension_semantics, vmem_limit_bytes=VMEM_LIMIT)


def _run_stages(*stage_gens):
    results = [None] * len(stage_gens)
    live = list(range(len(stage_gens)))
    while live:
        for i in list(live):
            try:
                next(stage_gens[i])
            except StopIteration as done:
                results[i] = done.value
                live.remove(i)
    return results


def _ffn_stages(x, gpre_ref, gpost_ref, wgu_ref, wd_ref):
    d_ff = wd_ref.shape[0]
    h = _rms(x, gpre_ref[...], NORM_EPS).astype(BF16)
    acc = None
    for c in range(d_ff // FF_CHUNK):
        yield
        lo = c * FF_CHUNK
        g = _dot(h, wgu_ref[:, lo:lo + FF_CHUNK])
        u = _dot(h, wgu_ref[:, d_ff + lo:d_ff + lo + FF_CHUNK])
        a = (g * _sigmoid(g) * u).astype(BF16)
        part = _dot(a, wd_ref[lo:lo + FF_CHUNK, :])
        acc = part if acc is None else acc + part
    yield
    return x + _rms(acc, gpost_ref[...], NORM_EPS)


def _ffn_specs(d, d_ff):
    return [_const_spec((1, d)), _const_spec((1, d)), _const_spec((d, 2 * d_ff)), _const_spec((d_ff, d))]


def _proj_ffn_kernel(a_ref, wo_ref, gmix_ref, x_ref, gpre_ref, gpost_ref, wgu_ref, wd_ref, o_ref):
    x = x_ref[...] + _rms(_dot(a_ref[...], wo_ref[...]), gmix_ref[...], NORM_EPS)
    (o_ref[...],) = _run_stages(_ffn_stages(x, gpre_ref, gpost_ref, wgu_ref, wd_ref))


def _proj_ffn(a, w_o, g_mix_post, x, ffn_params):
    t, d = x.shape
    d_ff = ffn_params[3].shape[0]
    tm = TOKEN_TILE
    row = pl.BlockSpec((tm, d), lambda i: (i, 0))
    return pl.pallas_call(
        _proj_ffn_kernel,
        grid=(t // tm,),
        in_specs=[row, _const_spec((d, d)), _const_spec((1, d)), row] + _ffn_specs(d, d_ff),
        out_specs=row,
        out_shape=jax.ShapeDtypeStruct((t, d), F32),
        compiler_params=_params(("parallel",)),
        name="attn_proj_ffn",
    )(a, w_o, g_mix_post, x, *ffn_params)


def _mixer_ffn_kernel(*refs, mixer_fn, n_mixer_refs, n_tiles):
    mixer_refs = refs[:n_mixer_refs]
    gpre_ref, gpost_ref, wgu_ref, wd_ref, o_ref, ybuf = refs[n_mixer_refs:n_mixer_refs + 6]
    mixer_scratch = refs[n_mixer_refs + 6:]
    s = pl.program_id(0)

    @pl.when(s == 0)
    def _():
        ybuf[...] = jnp.zeros_like(ybuf)

    o_ref[...], ybuf[...] = _run_stages(
        _ffn_stages(ybuf[...], gpre_ref, gpost_ref, wgu_ref, wd_ref),
        mixer_fn(mixer_refs, mixer_scratch, jnp.minimum(s, n_tiles - 1)))


def _mixer_ffn(mixer_fn, mixer_args, mixer_specs, mixer_scratch, t, d, ffn_params, name):
    d_ff = ffn_params[3].shape[0]
    tm = TOKEN_TILE
    n_tiles = t // tm
    return pl.pallas_call(
        functools.partial(_mixer_ffn_kernel, mixer_fn=mixer_fn, n_mixer_refs=len(mixer_args), n_tiles=n_tiles),
        grid=(n_tiles + 1,),
        in_specs=list(mixer_specs) + _ffn_specs(d, d_ff),
        out_specs=pl.BlockSpec((tm, d), lambda s: (jnp.maximum(s - 1, 0), 0)),
        out_shape=jax.ShapeDtypeStruct((t, d), F32),
        scratch_shapes=[pltpu.VMEM((tm, d), F32)] + list(mixer_scratch),
        compiler_params=_params(("arbitrary",)),
        name=name,
    )(*mixer_args, *ffn_params)


def _rope_block(xb, cos, sin_signed, first_half):
    x_up = pltpu.roll(xb, LANES - DIFF_HEAD_DIM // 2, 1)
    x_dn = pltpu.roll(xb, DIFF_HEAD_DIM // 2, 1)
    return xb * cos + jnp.where(first_half, x_up, x_dn) * sin_signed


def _qkv_kernel(x_ref, g_ref, w_ref, cos_ref, sin_ref, q_ref, k_ref, v_ref):
    d = x_ref.shape[1]
    h = _rms(x_ref[...], g_ref[...], NORM_EPS).astype(BF16)
    cos = cos_ref[...]
    sin_signed = sin_ref[...]
    lane = lax.broadcasted_iota(jnp.int32, (1, LANES), 1)
    first_half = (lane % DIFF_HEAD_DIM) < (DIFF_HEAD_DIM // 2)
    q_scale = DIFF_HEAD_DIM ** -0.5 * LOG2_E
    q = _dot(h, w_ref[:, :d])
    for hb in range(d // LANES):
        lo = hb * LANES
        q_ref[:, lo:lo + LANES] = (_rope_block(q[:, lo:lo + LANES], cos, sin_signed, first_half) * q_scale).astype(BF16)
    k = _dot(h, w_ref[:, d:2 * d])
    for hb in range(d // LANES):
        lo = hb * LANES
        k_ref[:, lo:lo + LANES] = _rope_block(k[:, lo:lo + LANES], cos, sin_signed, first_half).astype(BF16)
    v_ref[...] = _dot(h, w_ref[:, 2 * d:]).astype(BF16)


def _qkv(x, g_pre, w_qkv, cos_tab, sin_tab, seq):
    t, d = x.shape
    tm = TOKEN_TILE
    row = pl.BlockSpec((tm, d), lambda i: (i, 0))
    tiles_per_seq = seq // tm
    tab = pl.BlockSpec((tm, LANES), lambda i: (i % tiles_per_seq, 0))
    out = jax.ShapeDtypeStruct((t, d), BF16)
    return pl.pallas_call(
        _qkv_kernel,
        grid=(t // tm,),
        in_specs=[row, _const_spec((1, d)), _const_spec((d, 3 * d)), tab, tab],
        out_specs=[row, row, row],
        out_shape=[out, out, out],
        compiler_params=_params(("parallel",)),
        name="qkv_rope",
    )(x, g_pre, w_qkv, cos_tab, sin_tab)


def _attn_kernel(lq1_ref, lk1_ref, lq2_ref, lk2_ref, gsub_ref, q_ref, k_ref, v_ref, o_ref, v_ones, *,
                 lambda_init, q_tile):
    lam = (jnp.exp(jnp.sum(lq1_ref[...] * lk1_ref[...], axis=-1, keepdims=True))
           - jnp.exp(jnp.sum(lq2_ref[...] * lk2_ref[...], axis=-1, keepdims=True))
           + lambda_init)
    k = k_ref[...]
    v_ones[:, :LANES] = v_ref[...]
    v_ones[:, LANES:] = jnp.ones(v_ref.shape, BF16)
    v1 = v_ones[...]
    lane = lax.broadcasted_iota(jnp.int32, (1, LANES), 1)

    def softmax_pv(qm):
        s = lax.dot_general(qm, k, (((1,), (1,)), ((), ())), preferred_element_type=F32)
        e = jnp.exp2(s - jnp.max(s, axis=-1, keepdims=True))
        ol = _dot(e.astype(BF16), v1)
        return ol[:, :LANES] / ol[:, LANES:LANES + 1]

    for t in range(q_ref.shape[0] // q_tile):
        q = q_ref[t * q_tile:(t + 1) * q_tile, :]
        zero = jnp.zeros_like(q)
        o1 = softmax_pv(jnp.where(lane < DIFF_HEAD_DIM, q, zero))
        o2 = softmax_pv(jnp.where(lane >= DIFF_HEAD_DIM, q, zero))
        o = o1 - lam * o2
        o = _rms(o, gsub_ref[...], LN_EPS) * (1.0 - lambda_init)
        o_ref[t * q_tile:(t + 1) * q_tile, :] = o.astype(BF16)


def _attention(q, k, v, lam_q1, lam_k1, lam_q2, lam_k2, g_subln, lambda_init):
    b, s, d = q.shape
    heads = d // LANES
    seq_spec = pl.BlockSpec((None, s, LANES), lambda bi, hi: (bi, 0, hi))
    return pl.pallas_call(
        functools.partial(_attn_kernel, lambda_init=lambda_init, q_tile=Q_TILE),
        grid=(b, heads),
        in_specs=[_const_spec((1, DIFF_HEAD_DIM))] * 4 + [_const_spec((1, LANES)), seq_spec, seq_spec, seq_spec],
        out_specs=seq_spec,
        out_shape=jax.ShapeDtypeStruct((b, s, d), BF16),
        scratch_shapes=[pltpu.VMEM((s, 2 * LANES), BF16)],
        compiler_params=_params(("parallel", "parallel")),
        name="diff_attn",
    )(lam_q1, lam_k1, lam_q2, lam_k2, g_subln, q, k, v)


def _conv_ffn_kernel(xp_ref, x_ref, xn_ref, gpre_ref, w1_ref, b1_ref, wdw_ref, bdw_ref, gcln_ref, bcln_ref,
                     w2_ref, b2_ref, gpost_ref, fpre_ref, fpost_ref, wgu_ref, wd_ref, o_ref,
                     ybuf, zbuf, cbuf, hbuf, accbuf, *, tiles_per_seq, n_tiles):
    ts, d = x_ref.shape
    n_chunks = wd_ref.shape[0]
    rows_per_iter = cbuf.shape[0] // n_chunks
    s = pl.program_id(0)
    tile = jnp.minimum(s, n_tiles - 1)
    at_seq_start = (tile % tiles_per_seq) == 0
    at_seq_end = (tile % tiles_per_seq) == tiles_per_seq - 1

    @pl.when(s == 0)
    def _():
        ybuf[...] = jnp.zeros_like(ybuf)
        zbuf[ts + 2 * HALO:, :] = jnp.zeros((zbuf.shape[0] - ts - 2 * HALO, d), F32)

    y_prev = ybuf[...]
    hbuf[...] = _rms(y_prev, fpre_ref[...], NORM_EPS).astype(BF16)
    accbuf[...] = jnp.zeros_like(accbuf)

    x = x_ref[...]
    h = _rms(jnp.concatenate([xp_ref[...], x, xn_ref[...]], axis=0), gpre_ref[...], NORM_EPS).astype(BF16)
    z = (_dot(h, w1_ref[:, :d]) + b1_ref[:, :d]) * _sigmoid(_dot(h, w1_ref[:, d:]) + b1_ref[:, d:])
    zbuf[HALO:HALO + ts, :] = z[HALO:HALO + ts, :]
    zbuf[0:HALO, :] = jnp.where(at_seq_start, 0.0, z[0:HALO, :])
    zbuf[HALO + ts:2 * HALO + ts, :] = jnp.where(at_seq_end, 0.0, z[HALO + ts:, :])

    def body(it, carry):
        hh = hbuf[...]
        g = _dot(hh, wgu_ref[it])
        u = _dot(hh, wgu_ref[n_chunks + it])
        a = (g * _sigmoid(g) * u).astype(BF16)
        accbuf[...] += _dot(a, wd_ref[it])

        r0 = pl.multiple_of(it * rows_per_iter, SUBLANES)
        for cb in range(d // LANES):
            c0 = cb * LANES
            acc = None
            for phase in range(SUBLANES):
                part = None
                for j in range(CONV_WIDTH):
                    off = HALO - CONV_PAD + j
                    if off % SUBLANES != phase:
                        continue
                    term = (wdw_ref[j:j + 1, c0:c0 + LANES]
                            * zbuf[pl.ds(r0 + (off - phase), rows_per_iter + SUBLANES), c0:c0 + LANES])
                    part = term if part is None else part + term
                if part is None:
                    continue
                part = part[phase:phase + rows_per_iter, :]
                acc = part if acc is None else acc + part
            cbuf[pl.ds(r0, rows_per_iter), c0:c0 + LANES] = acc + bdw_ref[:, c0:c0 + LANES]
        return carry

    lax.fori_loop(0, n_chunks, body, 0)

    o_ref[...] = y_prev + _rms(accbuf[...], fpost_ref[...], NORM_EPS)
    y = _layernorm(cbuf[0:ts, :], gcln_ref[...], bcln_ref[...], LN_EPS)
    y = (y * _sigmoid(y)).astype(BF16)
    m = _dot(y, w2_ref[...]) + b2_ref[...]
    ybuf[...] = x + _rms(m, gpost_ref[...], NORM_EPS)


def _conv_ffn_layer(x, seq, g_pre, w_pw1, b_pw1, w_dw, b_dw, g_cln, b_cln, w_pw2, b_pw2, g_post, ffn_params):
    t, d = x.shape
    ts = TOKEN_TILE
    n_tiles = t // ts
    halo_per_tile = ts // HALO
    n_halo_blocks = t // HALO
    g_ffn_pre, g_ffn_post, w_gate_up, w_down = ffn_params
    n_chunks = w_down.shape[0] // FF_CHUNK
    wgu = w_gate_up.reshape(d, 2 * n_chunks, FF_CHUNK).transpose(1, 0, 2)
    wd = w_down.reshape(n_chunks, FF_CHUNK, d)
    rows_per_iter = -(-ts // (n_chunks * SUBLANES)) * SUBLANES
    c_rows = rows_per_iter * n_chunks
    max_tap_base = (HALO - CONV_PAD + CONV_WIDTH - 1) // SUBLANES * SUBLANES
    z_rows = max(c_rows + SUBLANES + max_tap_base, ts + 2 * HALO + SUBLANES)

    def tile(s):
        return jnp.minimum(s, n_tiles - 1)

    row = pl.BlockSpec((ts, d), lambda s: (tile(s), 0))
    prev = pl.BlockSpec((HALO, d), lambda s: (jnp.maximum(tile(s) * halo_per_tile - 1, 0), 0))
    nxt = pl.BlockSpec((HALO, d), lambda s: (jnp.minimum((tile(s) + 1) * halo_per_tile, n_halo_blocks - 1), 0))
    specs = [prev, row, nxt, _const_spec((1, d)), _const_spec((d, 2 * d)), _const_spec((1, 2 * d)),
             _const_spec((CONV_WIDTH, d)), _const_spec((1, d)), _const_spec((1, d)), _const_spec((1, d)),
             _const_spec((d, d)), _const_spec((1, d)), _const_spec((1, d)),
             _const_spec((1, d)), _const_spec((1, d)), _const_spec((2 * n_chunks, d, FF_CHUNK)),
             _const_spec((n_chunks, FF_CHUNK, d))]
    return pl.pallas_call(
        functools.partial(_conv_ffn_kernel, tiles_per_seq=seq // ts, n_tiles=n_tiles),
        grid=(n_tiles + 1,),
        in_specs=specs,
        out_specs=pl.BlockSpec((ts, d), lambda s: (jnp.maximum(s - 1, 0), 0)),
        out_shape=jax.ShapeDtypeStruct((t, d), F32),
        scratch_shapes=[pltpu.VMEM((ts, d), F32), pltpu.VMEM((z_rows, d), F32), pltpu.VMEM((c_rows, d), F32),
                        pltpu.VMEM((ts, d), BF16), pltpu.VMEM((ts, d), F32)],
        compiler_params=_params(("arbitrary",)),
        name="conformer_conv_ffn",
    )(x, x, x, g_pre, w_pw1, b_pw1, w_dw, b_dw, g_cln, b_cln, w_pw2, b_pw2, g_post,
      g_ffn_pre, g_ffn_post, wgu, wd)


def _sg_mixer_tile(refs, scratch, tile):
    del tile
    x_ref, gpre_ref, wuv_ref, buv_ref, gsln_ref, bsln_ref, ws_ref, bs_ref, wo_ref, bo_ref, gpost_ref = refs
    (gated,) = scratch
    tm, d = x_ref.shape
    n_chunks = tm // CHUNK
    x = x_ref[...]
    h = _rms(x, gpre_ref[...], NORM_EPS).astype(BF16)
    u = jax.nn.gelu(_dot(h, wuv_ref[:, :d]) + buv_ref[:, :d])
    yield
    v = jax.nn.gelu(_dot(h, wuv_ref[:, d:]) + buv_ref[:, d:])
    v = _layernorm(v, gsln_ref[...], bsln_ref[...], LN_EPS).astype(BF16)
    for g in range(SG_GROUPS):
        yield
        cl = g * LANES
        b_g = bs_ref[g]
        v_g = jnp.concatenate([v[c * CHUNK:(c + 1) * CHUNK, cl:cl + LANES] for c in range(n_chunks)], axis=1)
        mixed = _dot(ws_ref[g], v_g)
        for c in range(n_chunks):
            rl = c * CHUNK
            m_c = mixed[:, c * LANES:(c + 1) * LANES] + b_g
            gated[rl:rl + CHUNK, cl:cl + LANES] = (u[rl:rl + CHUNK, cl:cl + LANES] * m_c).astype(BF16)
    yield
    m = _dot(gated[...], wo_ref[...]) + bo_ref[...]
    return x + _rms(m, gpost_ref[...], NORM_EPS)


def _sg_ffn_layer(x, g_pre, w_uv, b_uv, g_sln, b_sln, w_s, b_s_bcast, w_o, b_o, g_post, ffn_params):
    t, d = x.shape
    tm = TOKEN_TILE
    n_tiles = t // tm
    row = pl.BlockSpec((tm, d), lambda s: (jnp.minimum(s, n_tiles - 1), 0))
    specs = [row, _const_spec((1, d)), _const_spec((d, 2 * d)), _const_spec((1, 2 * d)),
             _const_spec((1, d)), _const_spec((1, d)), _const_spec((SG_GROUPS, CHUNK, CHUNK)),
             _const_spec((SG_GROUPS, CHUNK, LANES)), _const_spec((d, d)), _const_spec((1, d)),
             _const_spec((1, d))]
    args = (x, g_pre, w_uv, b_uv, g_sln, b_sln, w_s, b_s_bcast, w_o, b_o, g_post)
    return _mixer_ffn(_sg_mixer_tile, args, specs, [pltpu.VMEM((tm, d), BF16)], t, d, ffn_params,
                      "spatial_gating_ffn")


def _rope_tables(seq):
    pos = jnp.arange(seq, dtype=F32)
    inv_freq = 1.0 / (ROPE_THETA ** (jnp.arange(0, DIFF_HEAD_DIM, 2, dtype=F32) / DIFF_HEAD_DIM))
    ang = pos[:, None] * inv_freq[None, :]
    cos, sin = jnp.cos(ang), jnp.sin(ang)
    reps = LANES // DIFF_HEAD_DIM
    cos_tab = jnp.tile(jnp.concatenate([cos, cos], axis=-1), (1, reps))
    sin_tab = jnp.tile(jnp.concatenate([-sin, sin], axis=-1), (1, reps))
    return cos_tab, sin_tab


def _row(v):
    return v.reshape(1, -1)


def kernel(x_prompt, x_sample, l0_g_mix_pre, l0_g_mix_post, l0_w_qkv, l0_lam_q1, l0_lam_k1, l0_lam_q2, l0_lam_k2, l0_g_subln, l0_w_o, l0_g_ffn_pre, l0_g_ffn_post, l0_w_gate_up, l0_w_down, l1_g_mix_pre, l1_g_mix_post, l1_w_pw1, l1_b_pw1, l1_w_dw, l1_b_dw, l1_g_cln, l1_b_cln, l1_w_pw2, l1_b_pw2, l1_g_ffn_pre, l1_g_ffn_post, l1_w_gate_up, l1_w_down, l2_g_mix_pre, l2_g_mix_post, l2_w_uv, l2_b_uv, l2_g_sln, l2_b_sln, l2_w_s, l2_b_s, l2_w_o, l2_b_o, l2_g_ffn_pre, l2_g_ffn_post, l2_w_gate_up, l2_w_down, l3_g_mix_pre, l3_g_mix_post, l3_w_qkv, l3_lam_q1, l3_lam_k1, l3_lam_q2, l3_lam_k2, l3_g_subln, l3_w_o, l3_g_ffn_pre, l3_g_ffn_post, l3_w_gate_up, l3_w_down):
    p = dict(locals())
    depth = 4
    seq = x_prompt.shape[1]
    cos_tab, sin_tab = _rope_tables(seq)

    def w16(name):
        return p[name].astype(BF16)

    layers = []
    for i in range(depth):
        n = f"l{i}_"
        kind = i % N_MIXERS
        lp = {"kind": kind,
              "g_mix_pre": _row(p[n + "g_mix_pre"]), "g_mix_post": _row(p[n + "g_mix_post"]),
              "g_ffn_pre": _row(p[n + "g_ffn_pre"]), "g_ffn_post": _row(p[n + "g_ffn_post"]),
              "w_gate_up": w16(n + "w_gate_up"), "w_down": w16(n + "w_down")}
        if kind == 0:
            lp.update(w_qkv=w16(n + "w_qkv"), w_o=w16(n + "w_o"), g_subln=_row(p[n + "g_subln"]),
                      lam=[_row(p[n + k]) for k in ("lam_q1", "lam_k1", "lam_q2", "lam_k2")],
                      lambda_init=_lambda_init(i))
        elif kind == 1:
            lp.update(w_pw1=w16(n + "w_pw1"), b_pw1=_row(p[n + "b_pw1"]), w_dw=p[n + "w_dw"],
                      b_dw=_row(p[n + "b_dw"]), g_cln=_row(p[n + "g_cln"]), b_cln=_row(p[n + "b_cln"]),
                      w_pw2=w16(n + "w_pw2"), b_pw2=_row(p[n + "b_pw2"]))
        else:
            b_s = p[n + "b_s"]
            lp.update(w_uv=w16(n + "w_uv"), b_uv=_row(p[n + "b_uv"]), g_sln=_row(p[n + "g_sln"]),
                      b_sln=_row(p[n + "b_sln"]), w_s=w16(n + "w_s"),
                      b_s=jnp.broadcast_to(b_s[:, :, None], b_s.shape + (LANES,)),
                      w_o=w16(n + "w_o"), b_o=_row(p[n + "b_o"]))
        layers.append(lp)

    def trunk(x3):
        b, s, d = x3.shape
        x = x3.reshape(b * s, d)
        for lp in layers:
            ffn_params = (lp["g_ffn_pre"], lp["g_ffn_post"], lp["w_gate_up"], lp["w_down"])
            if lp["kind"] == 0:
                q, k, v = _qkv(x, lp["g_mix_pre"], lp["w_qkv"], cos_tab, sin_tab, s)
                a = _attention(q.reshape(b, s, d), k.reshape(b, s, d), v.reshape(b, s, d),
                               *lp["lam"], lp["g_subln"], lp["lambda_init"])
                x = _proj_ffn(a.reshape(b * s, d), lp["w_o"], lp["g_mix_post"], x, ffn_params)
            elif lp["kind"] == 1:
                x = _conv_ffn_layer(x, s, lp["g_mix_pre"], lp["w_pw1"], lp["b_pw1"], lp["w_dw"], lp["b_dw"],
                                    lp["g_cln"], lp["b_cln"], lp["w_pw2"], lp["b_pw2"], lp["g_mix_post"],
                                    ffn_params)
            else:
                x = _sg_ffn_layer(x, lp["g_mix_pre"], lp["w_uv"], lp["b_uv"], lp["g_sln"], lp["b_sln"],
                                  lp["w_s"], lp["b_s"], lp["w_o"], lp["b_o"], lp["g_mix_post"], ffn_params)
        return x.reshape(b, s, d)

    return (trunk(x_prompt), trunk(x_sample))
```

```python
import functools
import math

import jax
import jax.numpy as jnp
from jax import lax
from jax.experimental import pallas as pl
from jax.experimental.pallas import tpu as pltpu

F32 = jnp.float32
BF16 = jnp.bfloat16

N_MIXERS = 3
DIFF_HEAD_DIM = 64
ROPE_THETA = 10000.0
CONV_WIDTH = 31
CONV_PAD = (CONV_WIDTH - 1) // 2
CHUNK = 128
SG_GROUPS = 8
NORM_EPS = 1e-6
LN_EPS = 1e-5
LOG2_E = 1.4426950408889634

LANES = 128
SUBLANES = 8
HALO = 16
VMEM_LIMIT = 56 * 1024 * 1024

TOKEN_TILE = 512
Q_TILE = 256
FF_CHUNK = 256


def _lambda_init(layer_idx):
    return 0.8 - 0.6 * math.exp(-0.3 * layer_idx)


def _rms(x, g, eps):
    return x * lax.rsqrt(jnp.mean(x * x, axis=-1, keepdims=True) + eps) * g


def _layernorm(x, g, b, eps):
    mu = jnp.mean(x, axis=-1, keepdims=True)
    xc = x - mu
    var = jnp.mean(xc * xc, axis=-1, keepdims=True)
    return xc * lax.rsqrt(var + eps) * g + b


def _sigmoid(x):
    return 1.0 / (1.0 + jnp.exp(-x))


def _dot(a, b):
    return jnp.dot(a, b, preferred_element_type=F32)


def _const_spec(shape):
    nd = len(shape)
    return pl.BlockSpec(shape, lambda *_: (0,) * nd, pipeline_mode=pl.Buffered(1))


def _params(dimension_semantics):
    return pltpu.CompilerParams(dimension_semantics=dimension_semantics, vmem_limit_bytes=VMEM_LIMIT)


def _run_stages(*stage_gens):
    results = [None] * len(stage_gens)
    live = list(range(len(stage_gens)))
    while live:
        for i in list(live):
            try:
                next(stage_gens[i])
            except StopIteration as done:
                results[i] = done.value
                live.remove(i)
    return results


def _ffn_stages(x, gpre_ref, gpost_ref, wgu_ref, wd_ref):
    d_ff = wd_ref.shape[0]
    h = _rms(x, gpre_ref[...], NORM_EPS).astype(BF16)
    acc = None
    for c in range(d_ff // FF_CHUNK):
        yield
        lo = c * FF_CHUNK
        g = _dot(h, wgu_ref[:, lo:lo + FF_CHUNK])
        u = _dot(h, wgu_ref[:, d_ff + lo:d_ff + lo + FF_CHUNK])
        a = (g * _sigmoid(g) * u).astype(BF16)
        part = _dot(a, wd_ref[lo:lo + FF_CHUNK, :])
        acc = part if acc is None else acc + part
    yield
    return x + _rms(acc, gpost_ref[...], NORM_EPS)


def _ffn_specs(d, d_ff):
    return [_const_spec((1, d)), _const_spec((1, d)), _const_spec((d, 2 * d_ff)), _const_spec((d_ff, d))]


def _proj_ffn_kernel(a_ref, wo_ref, gmix_ref, x_ref, gpre_ref, gpost_ref, wgu_ref, wd_ref, o_ref):
    x = x_ref[...] + _rms(_dot(a_ref[...], wo_ref[...]), gmix_ref[...], NORM_EPS)
    (o_ref[...],) = _run_stages(_ffn_stages(x, gpre_ref, gpost_ref, wgu_ref, wd_ref))


def _proj_ffn(a, w_o, g_mix_post, x, ffn_params):
    t, d = x.shape
    d_ff = ffn_params[3].shape[0]
    tm = TOKEN_TILE
    row = pl.BlockSpec((tm, d), lambda i: (i, 0))
    return pl.pallas_call(
        _proj_ffn_kernel,
        grid=(t // tm,),
        in_specs=[row, _const_spec((d, d)), _const_spec((1, d)), row] + _ffn_specs(d, d_ff),
        out_specs=row,
        out_shape=jax.ShapeDtypeStruct((t, d), F32),
        compiler_params=_params(("parallel",)),
        name="attn_proj_ffn",
    )(a, w_o, g_mix_post, x, *ffn_params)


def _mixer_ffn_kernel(*refs, mixer_fn, n_mixer_refs, n_tiles):
    mixer_refs = refs[:n_mixer_refs]
    gpre_ref, gpost_ref, wgu_ref, wd_ref, o_ref, ybuf = refs[n_mixer_refs:n_mixer_refs + 6]
    mixer_scratch = refs[n_mixer_refs + 6:]
    s = pl.program_id(0)

    @pl.when(s == 0)
    def _():
        ybuf[...] = jnp.zeros_like(ybuf)

    o_ref[...], ybuf[...] = _run_stages(
        _ffn_stages(ybuf[...], gpre_ref, gpost_ref, wgu_ref, wd_ref),
        mixer_fn(mixer_refs, mixer_scratch, jnp.minimum(s, n_tiles - 1)))


def _mixer_ffn(mixer_fn, mixer_args, mixer_specs, mixer_scratch, t, d, ffn_params, name):
    d_ff = ffn_params[3].shape[0]
    tm = TOKEN_TILE
    n_tiles = t // tm
    return pl.pallas_call(
        functools.partial(_mixer_ffn_kernel, mixer_fn=mixer_fn, n_mixer_refs=len(mixer_args), n_tiles=n_tiles),
        grid=(n_tiles + 1,),
        in_specs=list(mixer_specs) + _ffn_specs(d, d_ff),
        out_specs=pl.BlockSpec((tm, d), lambda s: (jnp.maximum(s - 1, 0), 0)),
        out_shape=jax.ShapeDtypeStruct((t, d), F32),
        scratch_shapes=[pltpu.VMEM((tm, d), F32)] + list(mixer_scratch),
        compiler_params=_params(("arbitrary",)),
        name=name,
    )(*mixer_args, *ffn_params)


def _rope_block(xb, cos, sin_signed, first_half):
    x_up = pltpu.roll(xb, LANES - DIFF_HEAD_DIM // 2, 1)
    x_dn = pltpu.roll(xb, DIFF_HEAD_DIM // 2, 1)
    return xb * cos + jnp.where(first_half, x_up, x_dn) * sin_signed


def _qkv_kernel(x_ref, g_ref, w_ref, cos_ref, sin_ref, q_ref, k_ref, v_ref):
    d = x_ref.shape[1]
    h = _rms(x_ref[...], g_ref[...], NORM_EPS).astype(BF16)
    cos = cos_ref[...]
    sin_signed = sin_ref[...]
    lane = lax.broadcasted_iota(jnp.int32, (1, LANES), 1)
    first_half = (lane % DIFF_HEAD_DIM) < (DIFF_HEAD_DIM // 2)
    q_scale = DIFF_HEAD_DIM ** -0.5 * LOG2_E
    q = _dot(h, w_ref[:, :d])
    for hb in range(d // LANES):
        lo = hb * LANES
        q_ref[:, lo:lo + LANES] = (_rope_block(q[:, lo:lo + LANES], cos, sin_signed, first_half) * q_scale).astype(BF16)
    k = _dot(h, w_ref[:, d:2 * d])
    for hb in range(d // LANES):
        lo = hb * LANES
        k_ref[:, lo:lo + LANES] = _rope_block(k[:, lo:lo + LANES], cos, sin_signed, first_half).astype(BF16)
    v_ref[...] = _dot(h, w_ref[:, 2 * d:]).astype(BF16)


def _qkv(x, g_pre, w_qkv, cos_tab, sin_tab, seq):
    t, d = x.shape
    tm = TOKEN_TILE
    row = pl.BlockSpec((tm, d), lambda i: (i, 0))
    tiles_per_seq = seq // tm
    tab = pl.BlockSpec((tm, LANES), lambda i: (i % tiles_per_seq, 0))
    out = jax.ShapeDtypeStruct((t, d), BF16)
    return pl.pallas_call(
        _qkv_kernel,
        grid=(t // tm,),
        in_specs=[row, _const_spec((1, d)), _const_spec((d, 3 * d)), tab, tab],
        out_specs=[row, row, row],
        out_shape=[out, out, out],
        compiler_params=_params(("parallel",)),
        name="qkv_rope",
    )(x, g_pre, w_qkv, cos_tab, sin_tab)


def _attn_kernel(lq1_ref, lk1_ref, lq2_ref, lk2_ref, gsub_ref, q_ref, k_ref, v_ref, o_ref, v_ones, *,
                 lambda_init, q_tile):
    lam = (jnp.exp(jnp.sum(lq1_ref[...] * lk1_ref[...], axis=-1, keepdims=True))
           - jnp.exp(jnp.sum(lq2_ref[...] * lk2_ref[...], axis=-1, keepdims=True))
           + lambda_init)
    k = k_ref[...]
    v_ones[:, :LANES] = v_ref[...]
    v_ones[:, LANES:] = jnp.ones(v_ref.shape, BF16)
    v1 = v_ones[...]
    lane = lax.broadcasted_iota(jnp.int32, (1, LANES), 1)

    def softmax_pv(qm):
        s = lax.dot_general(qm, k, (((1,), (1,)), ((), ())), preferred_element_type=F32)
        e = jnp.exp2(s - jnp.max(s, axis=-1, keepdims=True))
        ol = _dot(e.astype(BF16), v1)
        return ol[:, :LANES] / ol[:, LANES:LANES + 1]

    for t in range(q_ref.shape[0] // q_tile):
        q = q_ref[t * q_tile:(t + 1) * q_tile, :]
        zero = jnp.zeros_like(q)
        o1 = softmax_pv(jnp.where(lane < DIFF_HEAD_DIM, q, zero))
        o2 = softmax_pv(jnp.where(lane >= DIFF_HEAD_DIM, q, zero))
        o = o1 - lam * o2
        o = _rms(o, gsub_ref[...], LN_EPS) * (1.0 - lambda_init)
        o_ref[t * q_tile:(t + 1) * q_tile, :] = o.astype(BF16)


def _attention(q, k, v, lam_q1, lam_k1, lam_q2, lam_k2, g_subln, lambda_init):
    b, s, d = q.shape
    heads = d // LANES
    seq_spec = pl.BlockSpec((None, s, LANES), lambda bi, hi: (bi, 0, hi))
    return pl.pallas_call(
        functools.partial(_attn_kernel, lambda_init=lambda_init, q_tile=Q_TILE),
        grid=(b, heads),
        in_specs=[_const_spec((1, DIFF_HEAD_DIM))] * 4 + [_const_spec((1, LANES)), seq_spec, seq_spec, seq_spec],
        out_specs=seq_spec,
        out_shape=jax.ShapeDtypeStruct((b, s, d), BF16),
        scratch_shapes=[pltpu.VMEM((s, 2 * LANES), BF16)],
        compiler_params=_params(("parallel", "parallel")),
        name="diff_attn",
    )(lam_q1, lam_k1, lam_q2, lam_k2, g_subln, q, k, v)


def _conv_mixer_tile(refs, scratch, tile, *, tiles_per_seq):
    (xp_ref, x_ref, xn_ref, gpre_ref, w1_ref, b1_ref, wdw_ref, bdw_ref, gcln_ref, bcln_ref,
     w2_ref, b2_ref, gpost_ref) = refs
    (zbuf,) = scratch
    ts, d = x_ref.shape
    at_seq_start = (tile % tiles_per_seq) == 0
    at_seq_end = (tile % tiles_per_seq) == tiles_per_seq - 1

    x = x_ref[...]
    h = _rms(jnp.concatenate([xp_ref[...], x, xn_ref[...]], axis=0), gpre_ref[...], NORM_EPS).astype(BF16)
    a = _dot(h, w1_ref[:, :d]) + b1_ref[:, :d]
    yield
    z = a * _sigmoid(_dot(h, w1_ref[:, d:]) + b1_ref[:, d:])
    zbuf[HALO:HALO + ts, :] = z[HALO:HALO + ts, :]
    zbuf[0:HALO, :] = jnp.where(at_seq_start, 0.0, z[0:HALO, :])
    zbuf[HALO + ts:, :] = jnp.where(at_seq_end, 0.0, z[HALO + ts:, :])

    acc = None
    for phase in range(SUBLANES):
        part = None
        for j in range(CONV_WIDTH):
            off = HALO - CONV_PAD + j
            if off % SUBLANES != phase:
                continue
            base = off - phase
            term = wdw_ref[j:j + 1, :] * zbuf[base:base + ts + SUBLANES, :]
            part = term if part is None else part + term
        if part is None:
            continue
        yield
        part = part[phase:phase + ts, :]
        acc = part if acc is None else acc + part
    acc = acc + bdw_ref[...]

    y = _layernorm(acc, gcln_ref[...], bcln_ref[...], LN_EPS)
    y = (y * _sigmoid(y)).astype(BF16)
    yield
    m = _dot(y, w2_ref[...]) + b2_ref[...]
    return x + _rms(m, gpost_ref[...], NORM_EPS)


def _conv_ffn_layer(x, seq, g_pre, w_pw1, b_pw1, w_dw, b_dw, g_cln, b_cln, w_pw2, b_pw2, g_post, ffn_params):
    t, d = x.shape
    ts = TOKEN_TILE
    n_tiles = t // ts
    halo_per_tile = ts // HALO
    n_halo_blocks = t // HALO

    def tile(s):
        return jnp.minimum(s, n_tiles - 1)

    row = pl.BlockSpec((ts, d), lambda s: (tile(s), 0))
    prev = pl.BlockSpec((HALO, d), lambda s: (jnp.maximum(tile(s) * halo_per_tile - 1, 0), 0))
    nxt = pl.BlockSpec((HALO, d), lambda s: (jnp.minimum((tile(s) + 1) * halo_per_tile, n_halo_blocks - 1), 0))
    specs = [prev, row, nxt, _const_spec((1, d)), _const_spec((d, 2 * d)), _const_spec((1, 2 * d)),
             _const_spec((CONV_WIDTH, d)), _const_spec((1, d)), _const_spec((1, d)), _const_spec((1, d)),
             _const_spec((d, d)), _const_spec((1, d)), _const_spec((1, d))]
    args = (x, x, x, g_pre, w_pw1, b_pw1, w_dw, b_dw, g_cln, b_cln, w_pw2, b_pw2, g_post)
    return _mixer_ffn(functools.partial(_conv_mixer_tile, tiles_per_seq=seq // ts), args, specs,
                      [pltpu.VMEM((ts + 2 * HALO, d), F32)], t, d, ffn_params, "conformer_conv_ffn")


def _sg_mixer_tile(refs, scratch, tile):
    del tile
    x_ref, gpre_ref, wuv_ref, buv_ref, gsln_ref, bsln_ref, ws_ref, bs_ref, wo_ref, bo_ref, gpost_ref = refs
    (gated,) = scratch
    tm, d = x_ref.shape
    n_chunks = tm // CHUNK
    x = x_ref[...]
    h = _rms(x, gpre_ref[...], NORM_EPS).astype(BF16)
    u = jax.nn.gelu(_dot(h, wuv_ref[:, :d]) + buv_ref[:, :d])
    yield
    v = jax.nn.gelu(_dot(h, wuv_ref[:, d:]) + buv_ref[:, d:])
    v = _layernorm(v, gsln_ref[...], bsln_ref[...], LN_EPS).astype(BF16)
    for g in range(SG_GROUPS):
        yield
        cl = g * LANES
        b_g = bs_ref[g]
        v_g = jnp.concatenate([v[c * CHUNK:(c + 1) * CHUNK, cl:cl + LANES] for c in range(n_chunks)], axis=1)
        mixed = _dot(ws_ref[g], v_g)
        for c in range(n_chunks):
            rl = c * CHUNK
            m_c = mixed[:, c * LANES:(c + 1) * LANES] + b_g
            gated[rl:rl + CHUNK, cl:cl + LANES] = (u[rl:rl + CHUNK, cl:cl + LANES] * m_c).astype(BF16)
    yield
    m = _dot(gated[...], wo_ref[...]) + bo_ref[...]
    return x + _rms(m, gpost_ref[...], NORM_EPS)


def _sg_ffn_layer(x, g_pre, w_uv, b_uv, g_sln, b_sln, w_s, b_s_bcast, w_o, b_o, g_post, ffn_params):
    t, d = x.shape
    tm = TOKEN_TILE
    n_tiles = t // tm
    row = pl.BlockSpec((tm, d), lambda s: (jnp.minimum(s, n_tiles - 1), 0))
    specs = [row, _const_spec((1, d)), _const_spec((d, 2 * d)), _const_spec((1, 2 * d)),
             _const_spec((1, d)), _const_spec((1, d)), _const_spec((SG_GROUPS, CHUNK, CHUNK)),
             _const_spec((SG_GROUPS, CHUNK, LANES)), _const_spec((d, d)), _const_spec((1, d)),
             _const_spec((1, d))]
    args = (x, g_pre, w_uv, b_uv, g_sln, b_sln, w_s, b_s_bcast, w_o, b_o, g_post)
    return _mixer_ffn(_sg_mixer_tile, args, specs, [pltpu.VMEM((tm, d), BF16)], t, d, ffn_params,
                      "spatial_gating_ffn")


def _rope_tables(seq):
    pos = jnp.arange(seq, dtype=F32)
    inv_freq = 1.0 / (ROPE_THETA ** (jnp.arange(0, DIFF_HEAD_DIM, 2, dtype=F32) / DIFF_HEAD_DIM))
    ang = pos[:, None] * inv_freq[None, :]
    cos, sin = jnp.cos(ang), jnp.sin(ang)
    reps = LANES // DIFF_HEAD_DIM
    cos_tab = jnp.tile(jnp.concatenate([cos, cos], axis=-1), (1, reps))
    sin_tab = jnp.tile(jnp.concatenate([-sin, sin], axis=-1), (1, reps))
    return cos_tab, sin_tab


def _row(v):
    return v.reshape(1, -1)


def kernel(x_prompt, x_sample, l0_g_mix_pre, l0_g_mix_post, l0_w_qkv, l0_lam_q1, l0_lam_k1, l0_lam_q2, l0_lam_k2, l0_g_subln, l0_w_o, l0_g_ffn_pre, l0_g_ffn_post, l0_w_gate_up, l0_w_down, l1_g_mix_pre, l1_g_mix_post, l1_w_pw1, l1_b_pw1, l1_w_dw, l1_b_dw, l1_g_cln, l1_b_cln, l1_w_pw2, l1_b_pw2, l1_g_ffn_pre, l1_g_ffn_post, l1_w_gate_up, l1_w_down, l2_g_mix_pre, l2_g_mix_post, l2_w_uv, l2_b_uv, l2_g_sln, l2_b_sln, l2_w_s, l2_b_s, l2_w_o, l2_b_o, l2_g_ffn_pre, l2_g_ffn_post, l2_w_gate_up, l2_w_down, l3_g_mix_pre, l3_g_mix_post, l3_w_qkv, l3_lam_q1, l3_lam_k1, l3_lam_q2, l3_lam_k2, l3_g_subln, l3_w_o, l3_g_ffn_pre, l3_g_ffn_post, l3_w_gate_up, l3_w_down):
    p = dict(locals())
    depth = 4
    seq = x_prompt.shape[1]
    cos_tab, sin_tab = _rope_tables(seq)

    def w16(name):
        return p[name].astype(BF16)

    layers = []
    for i in range(depth):
        n = f"l{i}_"
        kind = i % N_MIXERS
        lp = {"kind": kind,
              "g_mix_pre": _row(p[n + "g_mix_pre"]), "g_mix_post": _row(p[n + "g_mix_post"]),
              "g_ffn_pre": _row(p[n + "g_ffn_pre"]), "g_ffn_post": _row(p[n + "g_ffn_post"]),
              "w_gate_up": w16(n + "w_gate_up"), "w_down": w16(n + "w_down")}
        if kind == 0:
            lp.update(w_qkv=w16(n + "w_qkv"), w_o=w16(n + "w_o"), g_subln=_row(p[n + "g_subln"]),
                      lam=[_row(p[n + k]) for k in ("lam_q1", "lam_k1", "lam_q2", "lam_k2")],
                      lambda_init=_lambda_init(i))
        elif kind == 1:
            lp.update(w_pw1=w16(n + "w_pw1"), b_pw1=_row(p[n + "b_pw1"]), w_dw=p[n + "w_dw"],
                      b_dw=_row(p[n + "b_dw"]), g_cln=_row(p[n + "g_cln"]), b_cln=_row(p[n + "b_cln"]),
                      w_pw2=w16(n + "w_pw2"), b_pw2=_row(p[n + "b_pw2"]))
        else:
            b_s = p[n + "b_s"]
            lp.update(w_uv=w16(n + "w_uv"), b_uv=_row(p[n + "b_uv"]), g_sln=_row(p[n + "g_sln"]),
                      b_sln=_row(p[n + "b_sln"]), w_s=w16(n + "w_s"),
                      b_s=jnp.broadcast_to(b_s[:, :, None], b_s.shape + (LANES,)),
                      w_o=w16(n + "w_o"), b_o=_row(p[n + "b_o"]))
        layers.append(lp)

    def trunk(x3):
        b, s, d = x3.shape
        x = x3.reshape(b * s, d)
        for lp in layers:
            ffn_params = (lp["g_ffn_pre"], lp["g_ffn_post"], lp["w_gate_up"], lp["w_down"])
            if lp["kind"] == 0:
                q, k, v = _qkv(x, lp["g_mix_pre"], lp["w_qkv"], cos_tab, sin_tab, s)
                a = _attention(q.reshape(b, s, d), k.reshape(b, s, d), v.reshape(b, s, d),
                               *lp["lam"], lp["g_subln"], lp["lambda_init"])
                x = _proj_ffn(a.reshape(b * s, d), lp["w_o"], lp["g_mix_post"], x, ffn_params)
            elif lp["kind"] == 1:
                x = _conv_ffn_layer(x, s, lp["g_mix_pre"], lp["w_pw1"], lp["b_pw1"], lp["w_dw"], lp["b_dw"],
                                    lp["g_cln"], lp["b_cln"], lp["w_pw2"], lp["b_pw2"], lp["g_mix_post"],
                                    ffn_params)
            else:
                x = _sg_ffn_layer(x, lp["g_mix_pre"], lp["w_uv"], lp["b_uv"], lp["g_sln"], lp["b_sln"],
                                  lp["w_s"], lp["b_s"], lp["w_o"], lp["b_o"], lp["g_mix_post"], ffn_params)
        return x.reshape(b, s, d)

    return (trunk(x_prompt), trunk(x_sample))
```

```python
import functools
import math

import jax
import jax.numpy as jnp
from jax import lax
from jax.experimental import pallas as pl
from jax.experimental.pallas import tpu as pltpu

F32 = jnp.float32
BF16 = jnp.bfloat16

N_MIXERS = 3
DIFF_HEAD_DIM = 64
ROPE_THETA = 10000.0
CONV_WIDTH = 31
CONV_PAD = (CONV_WIDTH - 1) // 2
CHUNK = 128
SG_GROUPS = 8
NORM_EPS = 1e-6
LN_EPS = 1e-5
LOG2_E = 1.4426950408889634

LANES = 128
SUBLANES = 8
HALO = 16
VMEM_LIMIT = 56 * 1024 * 1024

TOKEN_TILE = 512
Q_TILE = 256
FF_CHUNK = 256


def _lambda_init(layer_idx):
    return 0.8 - 0.6 * math.exp(-0.3 * layer_idx)


def _rms(x, g, eps):
    return x * lax.rsqrt(jnp.mean(x * x, axis=-1, keepdims=True) + eps) * g


def _layernorm(x, g, b, eps):
    mu = jnp.mean(x, axis=-1, keepdims=True)
    xc = x - mu
    var = jnp.mean(xc * xc, axis=-1, keepdims=True)
    return xc * lax.rsqrt(var + eps) * g + b


def _sigmoid(x):
    return 1.0 / (1.0 + jnp.exp(-x))


def _dot(a, b):
    return jnp.dot(a, b, preferred_element_type=F32)


def _const_spec(shape):
    nd = len(shape)
    return pl.BlockSpec(shape, lambda *_: (0,) * nd, pipeline_mode=pl.Buffered(1))


def _params(dimension_semantics):
    return pltpu.CompilerParams(dimension_semantics=dimension_semantics, vmem_limit_bytes=VMEM_LIMIT)


def _zero_after(dep):
    bits = lax.bitcast_convert_type(dep, jnp.uint32)
    bits = lax.shift_right_logical(lax.shift_right_logical(bits, jnp.uint32(16)), jnp.uint32(16))
    return lax.bitcast_convert_type(bits, F32)[0:1, :]


def _gated(row, dep):
    return row if dep is None else row + _zero_after(dep)


def _run_stages(*stage_gens, lockstep=False):
    n = len(stage_gens)
    assert not lockstep or n == 2
    pieces = [None] * n
    results = [None] * n
    started = [False] * n
    live = list(range(n))
    while live:
        previous = list(pieces)
        for i in list(live):
            dep = previous[1 - i] if lockstep else None
            try:
                pieces[i] = stage_gens[i].send(dep) if started[i] else next(stage_gens[i])
                started[i] = True
            except StopIteration as done:
                results[i] = done.value
                live.remove(i)
    return results


def _ffn_stages(x, gpre_ref, gpost_ref, wgu_ref, wd_ref):
    d_ff = wd_ref.shape[0]
    tm = x.shape[0]
    h = _rms(x, gpre_ref[...], NORM_EPS).astype(BF16)
    acc = None
    dep = yield h[tm - 2 * SUBLANES:, :].astype(F32)[SUBLANES:, :]
    for c in range(d_ff // FF_CHUNK):
        lo = c * FF_CHUNK
        wg = wgu_ref[:, lo:lo + FF_CHUNK]
        wu = wgu_ref[:, d_ff + lo:d_ff + lo + FF_CHUNK]
        if dep is not None:
            zero = _zero_after(dep)[:, :FF_CHUNK].astype(BF16)
            wg = wg + zero
            wu = wu + zero
        g = _dot(h, wg)
        u = _dot(h, wu)
        a = (g * _sigmoid(g) * u).astype(BF16)
        part = _dot(a, wd_ref[lo:lo + FF_CHUNK, :])
        acc = part if acc is None else acc + part
        dep = yield part[tm - SUBLANES:, :]
    g_post = gpost_ref[...]
    if dep is not None:
        g_post = g_post + _zero_after(dep)
    return x + _rms(acc, g_post, NORM_EPS)


def _ffn_specs(d, d_ff):
    return [_const_spec((1, d)), _const_spec((1, d)), _const_spec((d, 2 * d_ff)), _const_spec((d_ff, d))]


def _proj_ffn_kernel(a_ref, wo_ref, gmix_ref, x_ref, gpre_ref, gpost_ref, wgu_ref, wd_ref, o_ref):
    x = x_ref[...] + _rms(_dot(a_ref[...], wo_ref[...]), gmix_ref[...], NORM_EPS)
    (o_ref[...],) = _run_stages(_ffn_stages(x, gpre_ref, gpost_ref, wgu_ref, wd_ref))


def _proj_ffn(a, w_o, g_mix_post, x, ffn_params):
    t, d = x.shape
    d_ff = ffn_params[3].shape[0]
    tm = TOKEN_TILE
    row = pl.BlockSpec((tm, d), lambda i: (i, 0))
    return pl.pallas_call(
        _proj_ffn_kernel,
        grid=(t // tm,),
        in_specs=[row, _const_spec((d, d)), _const_spec((1, d)), row] + _ffn_specs(d, d_ff),
        out_specs=row,
        out_shape=jax.ShapeDtypeStruct((t, d), F32),
        compiler_params=_params(("parallel",)),
        name="attn_proj_ffn",
    )(a, w_o, g_mix_post, x, *ffn_params)


def _mixer_ffn_kernel(*refs, mixer_fn, n_mixer_refs, n_tiles, lockstep):
    mixer_refs = refs[:n_mixer_refs]
    gpre_ref, gpost_ref, wgu_ref, wd_ref, o_ref, ybuf = refs[n_mixer_refs:n_mixer_refs + 6]
    mixer_scratch = refs[n_mixer_refs + 6:]
    s = pl.program_id(0)

    @pl.when(s == 0)
    def _():
        ybuf[...] = jnp.zeros_like(ybuf)

    o_ref[...], ybuf[...] = _run_stages(
        _ffn_stages(ybuf[...], gpre_ref, gpost_ref, wgu_ref, wd_ref),
        mixer_fn(mixer_refs, mixer_scratch, jnp.minimum(s, n_tiles - 1)), lockstep=lockstep)


def _mixer_ffn(mixer_fn, mixer_args, mixer_specs, mixer_scratch, t, d, ffn_params, name, lockstep):
    d_ff = ffn_params[3].shape[0]
    tm = TOKEN_TILE
    n_tiles = t // tm
    return pl.pallas_call(
        functools.partial(_mixer_ffn_kernel, mixer_fn=mixer_fn, n_mixer_refs=len(mixer_args), n_tiles=n_tiles,
                          lockstep=lockstep),
        grid=(n_tiles + 1,),
        in_specs=list(mixer_specs) + _ffn_specs(d, d_ff),
        out_specs=pl.BlockSpec((tm, d), lambda s: (jnp.maximum(s - 1, 0), 0)),
        out_shape=jax.ShapeDtypeStruct((t, d), F32),
        scratch_shapes=[pltpu.VMEM((tm, d), F32)] + list(mixer_scratch),
        compiler_params=_params(("arbitrary",)),
        name=name,
    )(*mixer_args, *ffn_params)


def _rope_block(xb, cos, sin_signed, first_half):
    x_up = pltpu.roll(xb, LANES - DIFF_HEAD_DIM // 2, 1)
    x_dn = pltpu.roll(xb, DIFF_HEAD_DIM // 2, 1)
    return xb * cos + jnp.where(first_half, x_up, x_dn) * sin_signed


def _qkv_kernel(x_ref, g_ref, w_ref, cos_ref, sin_ref, q_ref, k_ref, v_ref):
    d = x_ref.shape[1]
    h = _rms(x_ref[...], g_ref[...], NORM_EPS).astype(BF16)
    cos = cos_ref[...]
    sin_signed = sin_ref[...]
    lane = lax.broadcasted_iota(jnp.int32, (1, LANES), 1)
    first_half = (lane % DIFF_HEAD_DIM) < (DIFF_HEAD_DIM // 2)
    q_scale = DIFF_HEAD_DIM ** -0.5 * LOG2_E
    q = _dot(h, w_ref[:, :d])
    for hb in range(d // LANES):
        lo = hb * LANES
        q_ref[:, lo:lo + LANES] = (_rope_block(q[:, lo:lo + LANES], cos, sin_signed, first_half) * q_scale).astype(BF16)
    k = _dot(h, w_ref[:, d:2 * d])
    for hb in range(d // LANES):
        lo = hb * LANES
        k_ref[:, lo:lo + LANES] = _rope_block(k[:, lo:lo + LANES], cos, sin_signed, first_half).astype(BF16)
    v_ref[...] = _dot(h, w_ref[:, 2 * d:]).astype(BF16)


def _qkv(x, g_pre, w_qkv, cos_tab, sin_tab, seq):
    t, d = x.shape
    tm = TOKEN_TILE
    row = pl.BlockSpec((tm, d), lambda i: (i, 0))
    tiles_per_seq = seq // tm
    tab = pl.BlockSpec((tm, LANES), lambda i: (i % tiles_per_seq, 0))
    out = jax.ShapeDtypeStruct((t, d), BF16)
    return pl.pallas_call(
        _qkv_kernel,
        grid=(t // tm,),
        in_specs=[row, _const_spec((1, d)), _const_spec((d, 3 * d)), tab, tab],
        out_specs=[row, row, row],
        out_shape=[out, out, out],
        compiler_params=_params(("parallel",)),
        name="qkv_rope",
    )(x, g_pre, w_qkv, cos_tab, sin_tab)


def _attn_kernel(lq1_ref, lk1_ref, lq2_ref, lk2_ref, gsub_ref, q_ref, k_ref, v_ref, o_ref, v_ones, *,
                 lambda_init, q_tile):
    lam = (jnp.exp(jnp.sum(lq1_ref[...] * lk1_ref[...], axis=-1, keepdims=True))
           - jnp.exp(jnp.sum(lq2_ref[...] * lk2_ref[...], axis=-1, keepdims=True))
           + lambda_init)
    k = k_ref[...]
    v_ones[:, :LANES] = v_ref[...]
    v_ones[:, LANES:] = jnp.ones(v_ref.shape, BF16)
    v1 = v_ones[...]
    lane = lax.broadcasted_iota(jnp.int32, (1, LANES), 1)

    def softmax_pv(qm):
        s = lax.dot_general(qm, k, (((1,), (1,)), ((), ())), preferred_element_type=F32)
        e = jnp.exp2(s - jnp.max(s, axis=-1, keepdims=True))
        ol = _dot(e.astype(BF16), v1)
        return ol[:, :LANES] / ol[:, LANES:LANES + 1]

    for t in range(q_ref.shape[0] // q_tile):
        q = q_ref[t * q_tile:(t + 1) * q_tile, :]
        zero = jnp.zeros_like(q)
        o1 = softmax_pv(jnp.where(lane < DIFF_HEAD_DIM, q, zero))
        o2 = softmax_pv(jnp.where(lane >= DIFF_HEAD_DIM, q, zero))
        o = o1 - lam * o2
        o = _rms(o, gsub_ref[...], LN_EPS) * (1.0 - lambda_init)
        o_ref[t * q_tile:(t + 1) * q_tile, :] = o.astype(BF16)


def _attention(q, k, v, lam_q1, lam_k1, lam_q2, lam_k2, g_subln, lambda_init):
    b, s, d = q.shape
    heads = d // LANES
    seq_spec = pl.BlockSpec((None, s, LANES), lambda bi, hi: (bi, 0, hi))
    return pl.pallas_call(
        functools.partial(_attn_kernel, lambda_init=lambda_init, q_tile=Q_TILE),
        grid=(b, heads),
        in_specs=[_const_spec((1, DIFF_HEAD_DIM))] * 4 + [_const_spec((1, LANES)), seq_spec, seq_spec, seq_spec],
        out_specs=seq_spec,
        out_shape=jax.ShapeDtypeStruct((b, s, d), BF16),
        scratch_shapes=[pltpu.VMEM((s, 2 * LANES), BF16)],
        compiler_params=_params(("parallel", "parallel")),
        name="diff_attn",
    )(lam_q1, lam_k1, lam_q2, lam_k2, g_subln, q, k, v)


def _conv_mixer_tile(refs, scratch, tile, *, tiles_per_seq):
    (xp_ref, x_ref, xn_ref, gpre_ref, w1_ref, b1_ref, wdw_ref, bdw_ref, gcln_ref, bcln_ref,
     w2_ref, b2_ref, gpost_ref) = refs
    (zbuf,) = scratch
    ts, d = x_ref.shape
    at_seq_start = (tile % tiles_per_seq) == 0
    at_seq_end = (tile % tiles_per_seq) == tiles_per_seq - 1

    x = x_ref[...]
    h = _rms(jnp.concatenate([xp_ref[...], x, xn_ref[...]], axis=0), gpre_ref[...], NORM_EPS).astype(BF16)
    a = _dot(h, w1_ref[:, :d]) + b1_ref[:, :d]
    dep = yield a[ts + 2 * HALO - SUBLANES:, :]
    z = a * _sigmoid(_dot(h, w1_ref[:, d:]) + _gated(b1_ref[:, d:], dep))
    zbuf[HALO:HALO + ts, :] = z[HALO:HALO + ts, :]
    zbuf[0:HALO, :] = jnp.where(at_seq_start, 0.0, z[0:HALO, :])
    zbuf[HALO + ts:, :] = jnp.where(at_seq_end, 0.0, z[HALO + ts:, :])
    dep = yield z[ts + 2 * HALO - SUBLANES:, :]

    acc = None
    for phase in range(SUBLANES):
        part = None
        for j in range(CONV_WIDTH):
            off = HALO - CONV_PAD + j
            if off % SUBLANES != phase:
                continue
            base = off - phase
            term = _gated(wdw_ref[j:j + 1, :], dep) * zbuf[base:base + ts + SUBLANES, :]
            part = term if part is None else part + term
        if part is None:
            continue
        part = part[phase:phase + ts, :]
        acc = part if acc is None else acc + part
        dep = yield acc[ts - SUBLANES:, :]
    acc = acc + _gated(bdw_ref[...], dep)

    y = _layernorm(acc, gcln_ref[...], bcln_ref[...], LN_EPS)
    y = y * _sigmoid(y)
    dep = yield y[ts - SUBLANES:, :]
    m = _dot(y.astype(BF16), w2_ref[...]) + _gated(b2_ref[...], dep)
    return x + _rms(m, gpost_ref[...], NORM_EPS)


def _conv_ffn_layer(x, seq, g_pre, w_pw1, b_pw1, w_dw, b_dw, g_cln, b_cln, w_pw2, b_pw2, g_post, ffn_params):
    t, d = x.shape
    ts = TOKEN_TILE
    n_tiles = t // ts
    halo_per_tile = ts // HALO
    n_halo_blocks = t // HALO

    def tile(s):
        return jnp.minimum(s, n_tiles - 1)

    row = pl.BlockSpec((ts, d), lambda s: (tile(s), 0))
    prev = pl.BlockSpec((HALO, d), lambda s: (jnp.maximum(tile(s) * halo_per_tile - 1, 0), 0))
    nxt = pl.BlockSpec((HALO, d), lambda s: (jnp.minimum((tile(s) + 1) * halo_per_tile, n_halo_blocks - 1), 0))
    specs = [prev, row, nxt, _const_spec((1, d)), _const_spec((d, 2 * d)), _const_spec((1, 2 * d)),
             _const_spec((CONV_WIDTH, d)), _const_spec((1, d)), _const_spec((1, d)), _const_spec((1, d)),
             _const_spec((d, d)), _const_spec((1, d)), _const_spec((1, d))]
    args = (x, x, x, g_pre, w_pw1, b_pw1, w_dw, b_dw, g_cln, b_cln, w_pw2, b_pw2, g_post)
    return _mixer_ffn(functools.partial(_conv_mixer_tile, tiles_per_seq=seq // ts), args, specs,
                      [pltpu.VMEM((ts + 2 * HALO, d), F32)], t, d, ffn_params, "conformer_conv_ffn", lockstep=True)


def _sg_mixer_tile(refs, scratch, tile):
    del tile
    x_ref, gpre_ref, wuv_ref, buv_ref, gsln_ref, bsln_ref, ws_ref, bs_ref, wo_ref, bo_ref, gpost_ref = refs
    (gated,) = scratch
    tm, d = x_ref.shape
    n_chunks = tm // CHUNK
    x = x_ref[...]
    h = _rms(x, gpre_ref[...], NORM_EPS).astype(BF16)
    u = jax.nn.gelu(_dot(h, wuv_ref[:, :d]) + buv_ref[:, :d])
    yield
    v = jax.nn.gelu(_dot(h, wuv_ref[:, d:]) + buv_ref[:, d:])
    v = _layernorm(v, gsln_ref[...], bsln_ref[...], LN_EPS).astype(BF16)
    for g in range(SG_GROUPS):
        yield
        cl = g * LANES
        b_g = bs_ref[g]
        v_g = jnp.concatenate([v[c * CHUNK:(c + 1) * CHUNK, cl:cl + LANES] for c in range(n_chunks)], axis=1)
        mixed = _dot(ws_ref[g], v_g)
        for c in range(n_chunks):
            rl = c * CHUNK
            m_c = mixed[:, c * LANES:(c + 1) * LANES] + b_g
            gated[rl:rl + CHUNK, cl:cl + LANES] = (u[rl:rl + CHUNK, cl:cl + LANES] * m_c).astype(BF16)
    yield
    m = _dot(gated[...], wo_ref[...]) + bo_ref[...]
    return x + _rms(m, gpost_ref[...], NORM_EPS)


def _sg_ffn_layer(x, g_pre, w_uv, b_uv, g_sln, b_sln, w_s, b_s_bcast, w_o, b_o, g_post, ffn_params):
    t, d = x.shape
    tm = TOKEN_TILE
    n_tiles = t // tm
    row = pl.BlockSpec((tm, d), lambda s: (jnp.minimum(s, n_tiles - 1), 0))
    specs = [row, _const_spec((1, d)), _const_spec((d, 2 * d)), _const_spec((1, 2 * d)),
             _const_spec((1, d)), _const_spec((1, d)), _const_spec((SG_GROUPS, CHUNK, CHUNK)),
             _const_spec((SG_GROUPS, CHUNK, LANES)), _const_spec((d, d)), _const_spec((1, d)),
             _const_spec((1, d))]
    args = (x, g_pre, w_uv, b_uv, g_sln, b_sln, w_s, b_s_bcast, w_o, b_o, g_post)
    return _mixer_ffn(_sg_mixer_tile, args, specs, [pltpu.VMEM((tm, d), BF16)], t, d, ffn_params,
                      "spatial_gating_ffn", lockstep=False)


def _rope_tables(seq):
    pos = jnp.arange(seq, dtype=F32)
    inv_freq = 1.0 / (ROPE_THETA ** (jnp.arange(0, DIFF_HEAD_DIM, 2, dtype=F32) / DIFF_HEAD_DIM))
    ang = pos[:, None] * inv_freq[None, :]
    cos, sin = jnp.cos(ang), jnp.sin(ang)
    reps = LANES // DIFF_HEAD_DIM
    cos_tab = jnp.tile(jnp.concatenate([cos, cos], axis=-1), (1, reps))
    sin_tab = jnp.tile(jnp.concatenate([-sin, sin], axis=-1), (1, reps))
    return cos_tab, sin_tab


def _row(v):
    return v.reshape(1, -1)


def kernel(x_prompt, x_sample, l0_g_mix_pre, l0_g_mix_post, l0_w_qkv, l0_lam_q1, l0_lam_k1, l0_lam_q2, l0_lam_k2, l0_g_subln, l0_w_o, l0_g_ffn_pre, l0_g_ffn_post, l0_w_gate_up, l0_w_down, l1_g_mix_pre, l1_g_mix_post, l1_w_pw1, l1_b_pw1, l1_w_dw, l1_b_dw, l1_g_cln, l1_b_cln, l1_w_pw2, l1_b_pw2, l1_g_ffn_pre, l1_g_ffn_post, l1_w_gate_up, l1_w_down, l2_g_mix_pre, l2_g_mix_post, l2_w_uv, l2_b_uv, l2_g_sln, l2_b_sln, l2_w_s, l2_b_s, l2_w_o, l2_b_o, l2_g_ffn_pre, l2_g_ffn_post, l2_w_gate_up, l2_w_down, l3_g_mix_pre, l3_g_mix_post, l3_w_qkv, l3_lam_q1, l3_lam_k1, l3_lam_q2, l3_lam_k2, l3_g_subln, l3_w_o, l3_g_ffn_pre, l3_g_ffn_post, l3_w_gate_up, l3_w_down):
    p = dict(locals())
    depth = 4
    seq = x_prompt.shape[1]
    cos_tab, sin_tab = _rope_tables(seq)

    def w16(name):
        return p[name].astype(BF16)

    layers = []
    for i in range(depth):
        n = f"l{i}_"
        kind = i % N_MIXERS
        lp = {"kind": kind,
              "g_mix_pre": _row(p[n + "g_mix_pre"]), "g_mix_post": _row(p[n + "g_mix_post"]),
              "g_ffn_pre": _row(p[n + "g_ffn_pre"]), "g_ffn_post": _row(p[n + "g_ffn_post"]),
              "w_gate_up": w16(n + "w_gate_up"), "w_down": w16(n + "w_down")}
        if kind == 0:
            lp.update(w_qkv=w16(n + "w_qkv"), w_o=w16(n + "w_o"), g_subln=_row(p[n + "g_subln"]),
                      lam=[_row(p[n + k]) for k in ("lam_q1", "lam_k1", "lam_q2", "lam_k2")],
                      lambda_init=_lambda_init(i))
        elif kind == 1:
            lp.update(w_pw1=w16(n + "w_pw1"), b_pw1=_row(p[n + "b_pw1"]), w_dw=p[n + "w_dw"],
                      b_dw=_row(p[n + "b_dw"]), g_cln=_row(p[n + "g_cln"]), b_cln=_row(p[n + "b_cln"]),
                      w_pw2=w16(n + "w_pw2"), b_pw2=_row(p[n + "b_pw2"]))
        else:
            b_s = p[n + "b_s"]
            lp.update(w_uv=w16(n + "w_uv"), b_uv=_row(p[n + "b_uv"]), g_sln=_row(p[n + "g_sln"]),
                      b_sln=_row(p[n + "b_sln"]), w_s=w16(n + "w_s"),
                      b_s=jnp.broadcast_to(b_s[:, :, None], b_s.shape + (LANES,)),
                      w_o=w16(n + "w_o"), b_o=_row(p[n + "b_o"]))
        layers.append(lp)

    def trunk(x3):
        b, s, d = x3.shape
        x = x3.reshape(b * s, d)
        for lp in layers:
            ffn_params = (lp["g_ffn_pre"], lp["g_ffn_post"], lp["w_gate_up"], lp["w_down"])
            if lp["kind"] == 0:
                q, k, v = _qkv(x, lp["g_mix_pre"], lp["w_qkv"], cos_tab, sin_tab, s)
                a = _attention(q.reshape(b, s, d), k.reshape(b, s, d), v.reshape(b, s, d),
                               *lp["lam"], lp["g_subln"], lp["lambda_init"])
                x = _proj_ffn(a.reshape(b * s, d), lp["w_o"], lp["g_mix_post"], x, ffn_params)
            elif lp["kind"] == 1:
                x = _conv_ffn_layer(x, s, lp["g_mix_pre"], lp["w_pw1"], lp["b_pw1"], lp["w_dw"], lp["b_dw"],
                                    lp["g_cln"], lp["b_cln"], lp["w_pw2"], lp["b_pw2"], lp["g_mix_post"],
                                    ffn_params)
            else:
                x = _sg_ffn_layer(x, lp["g_mix_pre"], lp["w_uv"], lp["b_uv"], lp["g_sln"], lp["b_sln"],
                                  lp["w_s"], lp["b_s"], lp["w_o"], lp["b_o"], lp["g_mix_post"], ffn_params)
        return x.reshape(b, s, d)

    return (trunk(x_prompt), trunk(x_sample))
```

```python
import functools
import math

import jax
import jax.numpy as jnp
from jax import lax
from jax.experimental import pallas as pl
from jax.experimental.pallas import tpu as pltpu

F32 = jnp.float32
BF16 = jnp.bfloat16

N_MIXERS = 3
DIFF_HEAD_DIM = 64
ROPE_THETA = 10000.0
CONV_WIDTH = 31
CONV_PAD = (CONV_WIDTH - 1) // 2
CHUNK = 128
SG_GROUPS = 8
NORM_EPS = 1e-6
LN_EPS = 1e-5
LOG2_E = 1.4426950408889634

LANES = 128
SUBLANES = 8
HALO = 16
VMEM_LIMIT = 56 * 1024 * 1024

TOKEN_TILE = 512
DENSE_TILE = 1024
Q_TILE = 256
FF_CHUNK = 256


def _lambda_init(layer_idx):
    return 0.8 - 0.6 * math.exp(-0.3 * layer_idx)


def _rms(x, g, eps):
    return x * lax.rsqrt(jnp.mean(x * x, axis=-1, keepdims=True) + eps) * g


def _layernorm(x, g, b, eps):
    mu = jnp.mean(x, axis=-1, keepdims=True)
    xc = x - mu
    var = jnp.mean(xc * xc, axis=-1, keepdims=True)
    return xc * lax.rsqrt(var + eps) * g + b


def _sigmoid(x):
    return 1.0 / (1.0 + jnp.exp(-x))


def _dot(a, b):
    return jnp.dot(a, b, preferred_element_type=F32)


def _const_spec(shape):
    nd = len(shape)
    return pl.BlockSpec(shape, lambda *_: (0,) * nd, pipeline_mode=pl.Buffered(1))


def _params(dimension_semantics):
    return pltpu.CompilerParams(dimension_semantics=dimension_semantics, vmem_limit_bytes=VMEM_LIMIT)


def _zero_after(dep):
    bits = lax.bitcast_convert_type(dep, jnp.uint32)
    bits = lax.shift_right_logical(lax.shift_right_logical(bits, jnp.uint32(16)), jnp.uint32(16))
    return lax.bitcast_convert_type(bits, F32)[0:1, :]


def _gated(row, dep):
    return row if dep is None else row + _zero_after(dep)


def _run_stages(*stage_gens, lockstep=False):
    n = len(stage_gens)
    assert not lockstep or n == 2
    pieces = [None] * n
    results = [None] * n
    started = [False] * n
    live = list(range(n))
    while live:
        previous = list(pieces)
        for i in list(live):
            dep = previous[1 - i] if lockstep else None
            try:
                pieces[i] = stage_gens[i].send(dep) if started[i] else next(stage_gens[i])
                started[i] = True
            except StopIteration as done:
                results[i] = done.value
                live.remove(i)
    return results


def _ffn_stages(x, gpre_ref, gpost_ref, wgu_ref, wd_ref):
    d_ff = wd_ref.shape[0]
    tm = x.shape[0]
    h = _rms(x, gpre_ref[...], NORM_EPS).astype(BF16)
    acc = None
    dep = yield h[tm - 2 * SUBLANES:, :].astype(F32)[SUBLANES:, :]
    for c in range(d_ff // FF_CHUNK):
        lo = c * FF_CHUNK
        wg = wgu_ref[:, lo:lo + FF_CHUNK]
        wu = wgu_ref[:, d_ff + lo:d_ff + lo + FF_CHUNK]
        if dep is not None:
            zero = _zero_after(dep)[:, :FF_CHUNK].astype(BF16)
            wg = wg + zero
            wu = wu + zero
        g = _dot(h, wg)
        u = _dot(h, wu)
        a = (g * _sigmoid(g) * u).astype(BF16)
        part = _dot(a, wd_ref[lo:lo + FF_CHUNK, :])
        acc = part if acc is None else acc + part
        dep = yield part[tm - SUBLANES:, :]
    g_post = gpost_ref[...]
    if dep is not None:
        g_post = g_post + _zero_after(dep)
    return x + _rms(acc, g_post, NORM_EPS)


def _ffn_specs(d, d_ff):
    return [_const_spec((1, d)), _const_spec((1, d)), _const_spec((d, 2 * d_ff)), _const_spec((d_ff, d))]


def _proj_ffn_kernel(a_ref, wo_ref, gmix_ref, x_ref, gpre_ref, gpost_ref, wgu_ref, wd_ref, o_ref):
    x = x_ref[...] + _rms(_dot(a_ref[...], wo_ref[...]), gmix_ref[...], NORM_EPS)
    (o_ref[...],) = _run_stages(_ffn_stages(x, gpre_ref, gpost_ref, wgu_ref, wd_ref))


def _proj_ffn(a, w_o, g_mix_post, x, ffn_params):
    t, d = x.shape
    d_ff = ffn_params[3].shape[0]
    tm = DENSE_TILE
    row = pl.BlockSpec((tm, d), lambda i: (i, 0))
    return pl.pallas_call(
        _proj_ffn_kernel,
        grid=(t // tm,),
        in_specs=[row, _const_spec((d, d)), _const_spec((1, d)), row] + _ffn_specs(d, d_ff),
        out_specs=row,
        out_shape=jax.ShapeDtypeStruct((t, d), F32),
        compiler_params=_params(("parallel",)),
        name="attn_proj_ffn",
    )(a, w_o, g_mix_post, x, *ffn_params)


def _mixer_ffn_kernel(*refs, mixer_fn, n_mixer_refs, n_tiles, lockstep):
    mixer_refs = refs[:n_mixer_refs]
    gpre_ref, gpost_ref, wgu_ref, wd_ref, o_ref, ybuf = refs[n_mixer_refs:n_mixer_refs + 6]
    mixer_scratch = refs[n_mixer_refs + 6:]
    s = pl.program_id(0)

    @pl.when(s == 0)
    def _():
        ybuf[...] = jnp.zeros_like(ybuf)

    o_ref[...], ybuf[...] = _run_stages(
        _ffn_stages(ybuf[...], gpre_ref, gpost_ref, wgu_ref, wd_ref),
        mixer_fn(mixer_refs, mixer_scratch, jnp.minimum(s, n_tiles - 1)), lockstep=lockstep)


def _mixer_ffn(mixer_fn, mixer_args, mixer_specs, mixer_scratch, t, d, ffn_params, name, lockstep):
    d_ff = ffn_params[3].shape[0]
    tm = TOKEN_TILE
    n_tiles = t // tm
    return pl.pallas_call(
        functools.partial(_mixer_ffn_kernel, mixer_fn=mixer_fn, n_mixer_refs=len(mixer_args), n_tiles=n_tiles,
                          lockstep=lockstep),
        grid=(n_tiles + 1,),
        in_specs=list(mixer_specs) + _ffn_specs(d, d_ff),
        out_specs=pl.BlockSpec((tm, d), lambda s: (jnp.maximum(s - 1, 0), 0)),
        out_shape=jax.ShapeDtypeStruct((t, d), F32),
        scratch_shapes=[pltpu.VMEM((tm, d), F32)] + list(mixer_scratch),
        compiler_params=_params(("arbitrary",)),
        name=name,
    )(*mixer_args, *ffn_params)


def _rope_block(xb, cos, sin_signed, first_half):
    x_up = pltpu.roll(xb, LANES - DIFF_HEAD_DIM // 2, 1)
    x_dn = pltpu.roll(xb, DIFF_HEAD_DIM // 2, 1)
    return xb * cos + jnp.where(first_half, x_up, x_dn) * sin_signed


def _qkv_kernel(x_ref, g_ref, w_ref, cos_ref, sin_ref, q_ref, k_ref, v_ref):
    d = x_ref.shape[1]
    h = _rms(x_ref[...], g_ref[...], NORM_EPS).astype(BF16)
    cos = cos_ref[...]
    sin_signed = sin_ref[...]
    lane = lax.broadcasted_iota(jnp.int32, (1, LANES), 1)
    first_half = (lane % DIFF_HEAD_DIM) < (DIFF_HEAD_DIM // 2)
    q_scale = DIFF_HEAD_DIM ** -0.5 * LOG2_E
    q = _dot(h, w_ref[:, :d])
    for hb in range(d // LANES):
        lo = hb * LANES
        q_ref[:, lo:lo + LANES] = (_rope_block(q[:, lo:lo + LANES], cos, sin_signed, first_half) * q_scale).astype(BF16)
    k = _dot(h, w_ref[:, d:2 * d])
    for hb in range(d // LANES):
        lo = hb * LANES
        k_ref[:, lo:lo + LANES] = _rope_block(k[:, lo:lo + LANES], cos, sin_signed, first_half).astype(BF16)
    v_ref[...] = _dot(h, w_ref[:, 2 * d:]).astype(BF16)


def _qkv(x, g_pre, w_qkv, cos_tab, sin_tab, seq):
    t, d = x.shape
    tm = DENSE_TILE
    row = pl.BlockSpec((tm, d), lambda i: (i, 0))
    tiles_per_seq = seq // tm
    tab = pl.BlockSpec((tm, LANES), lambda i: (i % tiles_per_seq, 0))
    out = jax.ShapeDtypeStruct((t, d), BF16)
    return pl.pallas_call(
        _qkv_kernel,
        grid=(t // tm,),
        in_specs=[row, _const_spec((1, d)), _const_spec((d, 3 * d)), tab, tab],
        out_specs=[row, row, row],
        out_shape=[out, out, out],
        compiler_params=_params(("parallel",)),
        name="qkv_rope",
    )(x, g_pre, w_qkv, cos_tab, sin_tab)


def _attn_kernel(lq1_ref, lk1_ref, lq2_ref, lk2_ref, gsub_ref, q_ref, k_ref, v_ref, o_ref, v_ones, *,
                 lambda_init, q_tile):
    lam = (jnp.exp(jnp.sum(lq1_ref[...] * lk1_ref[...], axis=-1, keepdims=True))
           - jnp.exp(jnp.sum(lq2_ref[...] * lk2_ref[...], axis=-1, keepdims=True))
           + lambda_init)
    k = k_ref[...]
    v_ones[:, :LANES] = v_ref[...]
    v_ones[:, LANES:] = jnp.ones(v_ref.shape, BF16)
    v1 = v_ones[...]
    lane = lax.broadcasted_iota(jnp.int32, (1, LANES), 1)

    def softmax_pv(qm):
        s = lax.dot_general(qm, k, (((1,), (1,)), ((), ())), preferred_element_type=F32)
        e = jnp.exp2(s - jnp.max(s, axis=-1, keepdims=True))
        ol = _dot(e.astype(BF16), v1)
        return ol[:, :LANES] / ol[:, LANES:LANES + 1]

    for t in range(q_ref.shape[0] // q_tile):
        q = q_ref[t * q_tile:(t + 1) * q_tile, :]
        zero = jnp.zeros_like(q)
        o1 = softmax_pv(jnp.where(lane < DIFF_HEAD_DIM, q, zero))
        o2 = softmax_pv(jnp.where(lane >= DIFF_HEAD_DIM, q, zero))
        o = o1 - lam * o2
        o = _rms(o, gsub_ref[...], LN_EPS) * (1.0 - lambda_init)
        o_ref[t * q_tile:(t + 1) * q_tile, :] = o.astype(BF16)


def _attention(q, k, v, lam_q1, lam_k1, lam_q2, lam_k2, g_subln, lambda_init):
    b, s, d = q.shape
    heads = d // LANES
    seq_spec = pl.BlockSpec((None, s, LANES), lambda bi, hi: (bi, 0, hi))
    return pl.pallas_call(
        functools.partial(_attn_kernel, lambda_init=lambda_init, q_tile=Q_TILE),
        grid=(b, heads),
        in_specs=[_const_spec((1, DIFF_HEAD_DIM))] * 4 + [_const_spec((1, LANES)), seq_spec, seq_spec, seq_spec],
        out_specs=seq_spec,
        out_shape=jax.ShapeDtypeStruct((b, s, d), BF16),
        scratch_shapes=[pltpu.VMEM((s, 2 * LANES), BF16)],
        compiler_params=_params(("parallel", "parallel")),
        name="diff_attn",
    )(lam_q1, lam_k1, lam_q2, lam_k2, g_subln, q, k, v)


def _conv_mixer_tile(refs, scratch, tile, *, tiles_per_seq):
    (xp_ref, x_ref, xn_ref, gpre_ref, w1_ref, b1_ref, wdw_ref, bdw_ref, gcln_ref, bcln_ref,
     w2_ref, b2_ref, gpost_ref) = refs
    (zbuf,) = scratch
    ts, d = x_ref.shape
    at_seq_start = (tile % tiles_per_seq) == 0
    at_seq_end = (tile % tiles_per_seq) == tiles_per_seq - 1

    x = x_ref[...]
    h = _rms(jnp.concatenate([xp_ref[...], x, xn_ref[...]], axis=0), gpre_ref[...], NORM_EPS).astype(BF16)
    a = _dot(h, w1_ref[:, :d]) + b1_ref[:, :d]
    dep = yield a[ts + 2 * HALO - SUBLANES:, :]
    z = a * _sigmoid(_dot(h, w1_ref[:, d:]) + _gated(b1_ref[:, d:], dep))
    zbuf[HALO:HALO + ts, :] = z[HALO:HALO + ts, :]
    zbuf[0:HALO, :] = jnp.where(at_seq_start, 0.0, z[0:HALO, :])
    zbuf[HALO + ts:, :] = jnp.where(at_seq_end, 0.0, z[HALO + ts:, :])
    dep = yield z[ts + 2 * HALO - SUBLANES:, :]

    acc = None
    for phase in range(SUBLANES):
        part = None
        for j in range(CONV_WIDTH):
            off = HALO - CONV_PAD + j
            if off % SUBLANES != phase:
                continue
            base = off - phase
            term = _gated(wdw_ref[j:j + 1, :], dep) * zbuf[base:base + ts + SUBLANES, :]
            part = term if part is None else part + term
        if part is None:
            continue
        part = part[phase:phase + ts, :]
        acc = part if acc is None else acc + part
        dep = yield acc[ts - SUBLANES:, :]
    acc = acc + _gated(bdw_ref[...], dep)

    y = _layernorm(acc, gcln_ref[...], bcln_ref[...], LN_EPS)
    y = y * _sigmoid(y)
    dep = yield y[ts - SUBLANES:, :]
    m = _dot(y.astype(BF16), w2_ref[...]) + _gated(b2_ref[...], dep)
    return x + _rms(m, gpost_ref[...], NORM_EPS)


def _conv_ffn_layer(x, seq, g_pre, w_pw1, b_pw1, w_dw, b_dw, g_cln, b_cln, w_pw2, b_pw2, g_post, ffn_params):
    t, d = x.shape
    ts = TOKEN_TILE
    n_tiles = t // ts
    halo_per_tile = ts // HALO
    n_halo_blocks = t // HALO

    def tile(s):
        return jnp.minimum(s, n_tiles - 1)

    row = pl.BlockSpec((ts, d), lambda s: (tile(s), 0))
    prev = pl.BlockSpec((HALO, d), lambda s: (jnp.maximum(tile(s) * halo_per_tile - 1, 0), 0))
    nxt = pl.BlockSpec((HALO, d), lambda s: (jnp.minimum((tile(s) + 1) * halo_per_tile, n_halo_blocks - 1), 0))
    specs = [prev, row, nxt, _const_spec((1, d)), _const_spec((d, 2 * d)), _const_spec((1, 2 * d)),
             _const_spec((CONV_WIDTH, d)), _const_spec((1, d)), _const_spec((1, d)), _const_spec((1, d)),
             _const_spec((d, d)), _const_spec((1, d)), _const_spec((1, d))]
    args = (x, x, x, g_pre, w_pw1, b_pw1, w_dw, b_dw, g_cln, b_cln, w_pw2, b_pw2, g_post)
    return _mixer_ffn(functools.partial(_conv_mixer_tile, tiles_per_seq=seq // ts), args, specs,
                      [pltpu.VMEM((ts + 2 * HALO, d), F32)], t, d, ffn_params, "conformer_conv_ffn", lockstep=True)


def _sg_mixer_tile(refs, scratch, tile):
    del tile
    x_ref, gpre_ref, wuv_ref, buv_ref, gsln_ref, bsln_ref, ws_ref, bs_ref, wo_ref, bo_ref, gpost_ref = refs
    (gated,) = scratch
    tm, d = x_ref.shape
    n_chunks = tm // CHUNK
    x = x_ref[...]
    h = _rms(x, gpre_ref[...], NORM_EPS).astype(BF16)
    u = jax.nn.gelu(_dot(h, wuv_ref[:, :d]) + buv_ref[:, :d])
    yield
    v = jax.nn.gelu(_dot(h, wuv_ref[:, d:]) + buv_ref[:, d:])
    v = _layernorm(v, gsln_ref[...], bsln_ref[...], LN_EPS).astype(BF16)
    for g in range(SG_GROUPS):
        yield
        cl = g * LANES
        b_g = bs_ref[g]
        v_g = jnp.concatenate([v[c * CHUNK:(c + 1) * CHUNK, cl:cl + LANES] for c in range(n_chunks)], axis=1)
        mixed = _dot(ws_ref[g], v_g)
        for c in range(n_chunks):
            rl = c * CHUNK
            m_c = mixed[:, c * LANES:(c + 1) * LANES] + b_g
            gated[rl:rl + CHUNK, cl:cl + LANES] = (u[rl:rl + CHUNK, cl:cl + LANES] * m_c).astype(BF16)
    yield
    m = _dot(gated[...], wo_ref[...]) + bo_ref[...]
    return x + _rms(m, gpost_ref[...], NORM_EPS)


def _sg_ffn_layer(x, g_pre, w_uv, b_uv, g_sln, b_sln, w_s, b_s_bcast, w_o, b_o, g_post, ffn_params):
    t, d = x.shape
    tm = TOKEN_TILE
    n_tiles = t // tm
    row = pl.BlockSpec((tm, d), lambda s: (jnp.minimum(s, n_tiles - 1), 0))
    specs = [row, _const_spec((1, d)), _const_spec((d, 2 * d)), _const_spec((1, 2 * d)),
             _const_spec((1, d)), _const_spec((1, d)), _const_spec((SG_GROUPS, CHUNK, CHUNK)),
             _const_spec((SG_GROUPS, CHUNK, LANES)), _const_spec((d, d)), _const_spec((1, d)),
             _const_spec((1, d))]
    args = (x, g_pre, w_uv, b_uv, g_sln, b_sln, w_s, b_s_bcast, w_o, b_o, g_post)
    return _mixer_ffn(_sg_mixer_tile, args, specs, [pltpu.VMEM((tm, d), BF16)], t, d, ffn_params,
                      "spatial_gating_ffn", lockstep=False)


def _rope_tables(seq):
    pos = jnp.arange(seq, dtype=F32)
    inv_freq = 1.0 / (ROPE_THETA ** (jnp.arange(0, DIFF_HEAD_DIM, 2, dtype=F32) / DIFF_HEAD_DIM))
    ang = pos[:, None] * inv_freq[None, :]
    cos, sin = jnp.cos(ang), jnp.sin(ang)
    reps = LANES // DIFF_HEAD_DIM
    cos_tab = jnp.tile(jnp.concatenate([cos, cos], axis=-1), (1, reps))
    sin_tab = jnp.tile(jnp.concatenate([-sin, sin], axis=-1), (1, reps))
    return cos_tab, sin_tab


def _row(v):
    return v.reshape(1, -1)


def kernel(x_prompt, x_sample, l0_g_mix_pre, l0_g_mix_post, l0_w_qkv, l0_lam_q1, l0_lam_k1, l0_lam_q2, l0_lam_k2, l0_g_subln, l0_w_o, l0_g_ffn_pre, l0_g_ffn_post, l0_w_gate_up, l0_w_down, l1_g_mix_pre, l1_g_mix_post, l1_w_pw1, l1_b_pw1, l1_w_dw, l1_b_dw, l1_g_cln, l1_b_cln, l1_w_pw2, l1_b_pw2, l1_g_ffn_pre, l1_g_ffn_post, l1_w_gate_up, l1_w_down, l2_g_mix_pre, l2_g_mix_post, l2_w_uv, l2_b_uv, l2_g_sln, l2_b_sln, l2_w_s, l2_b_s, l2_w_o, l2_b_o, l2_g_ffn_pre, l2_g_ffn_post, l2_w_gate_up, l2_w_down, l3_g_mix_pre, l3_g_mix_post, l3_w_qkv, l3_lam_q1, l3_lam_k1, l3_lam_q2, l3_lam_k2, l3_g_subln, l3_w_o, l3_g_ffn_pre, l3_g_ffn_post, l3_w_gate_up, l3_w_down):
    p = dict(locals())
    depth = 4
    seq = x_prompt.shape[1]
    cos_tab, sin_tab = _rope_tables(seq)

    def w16(name):
        return p[name].astype(BF16)

    layers = []
    for i in range(depth):
        n = f"l{i}_"
        kind = i % N_MIXERS
        lp = {"kind": kind,
              "g_mix_pre": _row(p[n + "g_mix_pre"]), "g_mix_post": _row(p[n + "g_mix_post"]),
              "g_ffn_pre": _row(p[n + "g_ffn_pre"]), "g_ffn_post": _row(p[n + "g_ffn_post"]),
              "w_gate_up": w16(n + "w_gate_up"), "w_down": w16(n + "w_down")}
        if kind == 0:
            lp.update(w_qkv=w16(n + "w_qkv"), w_o=w16(n + "w_o"), g_subln=_row(p[n + "g_subln"]),
                      lam=[_row(p[n + k]) for k in ("lam_q1", "lam_k1", "lam_q2", "lam_k2")],
                      lambda_init=_lambda_init(i))
        elif kind == 1:
            lp.update(w_pw1=w16(n + "w_pw1"), b_pw1=_row(p[n + "b_pw1"]), w_dw=p[n + "w_dw"],
                      b_dw=_row(p[n + "b_dw"]), g_cln=_row(p[n + "g_cln"]), b_cln=_row(p[n + "b_cln"]),
                      w_pw2=w16(n + "w_pw2"), b_pw2=_row(p[n + "b_pw2"]))
        else:
            b_s = p[n + "b_s"]
            lp.update(w_uv=w16(n + "w_uv"), b_uv=_row(p[n + "b_uv"]), g_sln=_row(p[n + "g_sln"]),
                      b_sln=_row(p[n + "b_sln"]), w_s=w16(n + "w_s"),
                      b_s=jnp.broadcast_to(b_s[:, :, None], b_s.shape + (LANES,)),
                      w_o=w16(n + "w_o"), b_o=_row(p[n + "b_o"]))
        layers.append(lp)

    def trunk(x3):
        b, s, d = x3.shape
        x = x3.reshape(b * s, d)
        for lp in layers:
            ffn_params = (lp["g_ffn_pre"], lp["g_ffn_post"], lp["w_gate_up"], lp["w_down"])
            if lp["kind"] == 0:
                q, k, v = _qkv(x, lp["g_mix_pre"], lp["w_qkv"], cos_tab, sin_tab, s)
                a = _attention(q.reshape(b, s, d), k.reshape(b, s, d), v.reshape(b, s, d),
                               *lp["lam"], lp["g_subln"], lp["lambda_init"])
                x = _proj_ffn(a.reshape(b * s, d), lp["w_o"], lp["g_mix_post"], x, ffn_params)
            elif lp["kind"] == 1:
                x = _conv_ffn_layer(x, s, lp["g_mix_pre"], lp["w_pw1"], lp["b_pw1"], lp["w_dw"], lp["b_dw"],
                                    lp["g_cln"], lp["b_cln"], lp["w_pw2"], lp["b_pw2"], lp["g_mix_post"],
                                    ffn_params)
            else:
                x = _sg_ffn_layer(x, lp["g_mix_pre"], lp["w_uv"], lp["b_uv"], lp["g_sln"], lp["b_sln"],
                                  lp["w_s"], lp["b_s"], lp["w_o"], lp["b_o"], lp["g_mix_post"], ffn_params)
        return x.reshape(b, s, d)

    return (trunk(x_prompt), trunk(x_sample))
```

```python
import functools
import math

import jax
import jax.numpy as jnp
from jax import lax
from jax.experimental import pallas as pl
from jax.experimental.pallas import tpu as pltpu

F32 = jnp.float32
BF16 = jnp.bfloat16

N_MIXERS = 3
DIFF_HEAD_DIM = 64
ROPE_THETA = 10000.0
CONV_WIDTH = 31
CONV_PAD = (CONV_WIDTH - 1) // 2
CHUNK = 128
SG_GROUPS = 8
NORM_EPS = 1e-6
LN_EPS = 1e-5
LOG2_E = 1.4426950408889634

LANES = 128
SUBLANES = 8
HALO = 16
VMEM_LIMIT = 56 * 1024 * 1024

TOKEN_TILE = 512
DENSE_TILE = 1024
Q_TILE = 256
FF_CHUNK = 256


def _lambda_init(layer_idx):
    return 0.8 - 0.6 * math.exp(-0.3 * layer_idx)


def _rms(x, g, eps):
    return x * lax.rsqrt(jnp.mean(x * x, axis=-1, keepdims=True) + eps) * g


def _layernorm(x, g, b, eps):
    mu = jnp.mean(x, axis=-1, keepdims=True)
    xc = x - mu
    var = jnp.mean(xc * xc, axis=-1, keepdims=True)
    return xc * lax.rsqrt(var + eps) * g + b


def _sigmoid(x):
    return 1.0 / (1.0 + jnp.exp(-x))


def _dot(a, b):
    return jnp.dot(a, b, preferred_element_type=F32)


def _const_spec(shape):
    nd = len(shape)
    return pl.BlockSpec(shape, lambda *_: (0,) * nd, pipeline_mode=pl.Buffered(1))


def _params(dimension_semantics):
    return pltpu.CompilerParams(dimension_semantics=dimension_semantics, vmem_limit_bytes=VMEM_LIMIT)


def _zero_after(dep):
    bits = lax.bitcast_convert_type(dep, jnp.uint32)
    bits = lax.shift_right_logical(lax.shift_right_logical(bits, jnp.uint32(16)), jnp.uint32(16))
    return lax.bitcast_convert_type(bits, F32)[0:1, :]


def _gated(row, dep):
    return row if dep is None else row + _zero_after(dep)


def _run_stages(*stage_gens, lockstep=False):
    n = len(stage_gens)
    assert not lockstep or n == 2
    pieces = [None] * n
    results = [None] * n
    started = [False] * n
    live = list(range(n))
    while live:
        previous = list(pieces)
        for i in list(live):
            dep = previous[1 - i] if lockstep else None
            try:
                pieces[i] = stage_gens[i].send(dep) if started[i] else next(stage_gens[i])
                started[i] = True
            except StopIteration as done:
                results[i] = done.value
                live.remove(i)
    return results


def _ffn_stages(x, gpre_ref, gpost_ref, wgu_ref, wd_ref):
    d_ff = wd_ref.shape[0]
    tm = x.shape[0]
    h = _rms(x, gpre_ref[...], NORM_EPS).astype(BF16)
    acc = None
    dep = yield h[tm - 2 * SUBLANES:, :].astype(F32)[SUBLANES:, :]
    for c in range(d_ff // FF_CHUNK):
        lo = c * FF_CHUNK
        wg = wgu_ref[:, lo:lo + FF_CHUNK]
        wu = wgu_ref[:, d_ff + lo:d_ff + lo + FF_CHUNK]
        if dep is not None:
            zero = _zero_after(dep)[:, :FF_CHUNK].astype(BF16)
            wg = wg + zero
            wu = wu + zero
        g = _dot(h, wg)
        u = _dot(h, wu)
        a = (g * _sigmoid(g) * u).astype(BF16)
        part = _dot(a, wd_ref[lo:lo + FF_CHUNK, :])
        acc = part if acc is None else acc + part
        dep = yield part[tm - SUBLANES:, :]
    g_post = gpost_ref[...]
    if dep is not None:
        g_post = g_post + _zero_after(dep)
    return x + _rms(acc, g_post, NORM_EPS)


def _ffn_specs(d, d_ff):
    return [_const_spec((1, d)), _const_spec((1, d)), _const_spec((d, 2 * d_ff)), _const_spec((d_ff, d))]


def _proj_ffn_stages(a_ref, wo_ref, gmix_ref, x_ref, ffn_refs, rows):
    x = x_ref[rows, :] + _rms(_dot(a_ref[rows, :], wo_ref[...]), gmix_ref[...], NORM_EPS)
    return (yield from _ffn_stages(x, *ffn_refs))


def _proj_ffn_kernel(a_ref, wo_ref, gmix_ref, x_ref, gpre_ref, gpost_ref, wgu_ref, wd_ref, o_ref):
    half = x_ref.shape[0] // 2
    ffn_refs = (gpre_ref, gpost_ref, wgu_ref, wd_ref)
    o_ref[:half, :], o_ref[half:, :] = _run_stages(
        _proj_ffn_stages(a_ref, wo_ref, gmix_ref, x_ref, ffn_refs, slice(0, half)),
        _proj_ffn_stages(a_ref, wo_ref, gmix_ref, x_ref, ffn_refs, slice(half, 2 * half)))


def _proj_ffn(a, w_o, g_mix_post, x, ffn_params):
    t, d = x.shape
    d_ff = ffn_params[3].shape[0]
    tm = DENSE_TILE
    row = pl.BlockSpec((tm, d), lambda i: (i, 0))
    return pl.pallas_call(
        _proj_ffn_kernel,
        grid=(t // tm,),
        in_specs=[row, _const_spec((d, d)), _const_spec((1, d)), row] + _ffn_specs(d, d_ff),
        out_specs=row,
        out_shape=jax.ShapeDtypeStruct((t, d), F32),
        compiler_params=_params(("parallel",)),
        name="attn_proj_ffn",
    )(a, w_o, g_mix_post, x, *ffn_params)


def _mixer_ffn_kernel(*refs, mixer_fn, n_mixer_refs, n_tiles, lockstep):
    mixer_refs = refs[:n_mixer_refs]
    gpre_ref, gpost_ref, wgu_ref, wd_ref, o_ref, ybuf = refs[n_mixer_refs:n_mixer_refs + 6]
    mixer_scratch = refs[n_mixer_refs + 6:]
    s = pl.program_id(0)

    @pl.when(s == 0)
    def _():
        ybuf[...] = jnp.zeros_like(ybuf)

    o_ref[...], ybuf[...] = _run_stages(
        _ffn_stages(ybuf[...], gpre_ref, gpost_ref, wgu_ref, wd_ref),
        mixer_fn(mixer_refs, mixer_scratch, jnp.minimum(s, n_tiles - 1)), lockstep=lockstep)


def _mixer_ffn(mixer_fn, mixer_args, mixer_specs, mixer_scratch, t, d, ffn_params, name, lockstep):
    d_ff = ffn_params[3].shape[0]
    tm = TOKEN_TILE
    n_tiles = t // tm
    return pl.pallas_call(
        functools.partial(_mixer_ffn_kernel, mixer_fn=mixer_fn, n_mixer_refs=len(mixer_args), n_tiles=n_tiles,
                          lockstep=lockstep),
        grid=(n_tiles + 1,),
        in_specs=list(mixer_specs) + _ffn_specs(d, d_ff),
        out_specs=pl.BlockSpec((tm, d), lambda s: (jnp.maximum(s - 1, 0), 0)),
        out_shape=jax.ShapeDtypeStruct((t, d), F32),
        scratch_shapes=[pltpu.VMEM((tm, d), F32)] + list(mixer_scratch),
        compiler_params=_params(("arbitrary",)),
        name=name,
    )(*mixer_args, *ffn_params)


def _rope_block(xb, cos, sin_signed, first_half):
    x_up = pltpu.roll(xb, LANES - DIFF_HEAD_DIM // 2, 1)
    x_dn = pltpu.roll(xb, DIFF_HEAD_DIM // 2, 1)
    return xb * cos + jnp.where(first_half, x_up, x_dn) * sin_signed


def _qkv_kernel(x_ref, g_ref, w_ref, cos_ref, sin_ref, q_ref, k_ref, v_ref):
    d = x_ref.shape[1]
    h = _rms(x_ref[...], g_ref[...], NORM_EPS).astype(BF16)
    cos = cos_ref[...]
    sin_signed = sin_ref[...]
    lane = lax.broadcasted_iota(jnp.int32, (1, LANES), 1)
    first_half = (lane % DIFF_HEAD_DIM) < (DIFF_HEAD_DIM // 2)
    q_scale = DIFF_HEAD_DIM ** -0.5 * LOG2_E
    q = _dot(h, w_ref[:, :d])
    for hb in range(d // LANES):
        lo = hb * LANES
        q_ref[:, lo:lo + LANES] = (_rope_block(q[:, lo:lo + LANES], cos, sin_signed, first_half) * q_scale).astype(BF16)
    k = _dot(h, w_ref[:, d:2 * d])
    for hb in range(d // LANES):
        lo = hb * LANES
        k_ref[:, lo:lo + LANES] = _rope_block(k[:, lo:lo + LANES], cos, sin_signed, first_half).astype(BF16)
    v_ref[...] = _dot(h, w_ref[:, 2 * d:]).astype(BF16)


def _qkv(x, g_pre, w_qkv, cos_tab, sin_tab, seq):
    t, d = x.shape
    tm = DENSE_TILE
    row = pl.BlockSpec((tm, d), lambda i: (i, 0))
    tiles_per_seq = seq // tm
    tab = pl.BlockSpec((tm, LANES), lambda i: (i % tiles_per_seq, 0))
    out = jax.ShapeDtypeStruct((t, d), BF16)
    return pl.pallas_call(
        _qkv_kernel,
        grid=(t // tm,),
        in_specs=[row, _const_spec((1, d)), _const_spec((d, 3 * d)), tab, tab],
        out_specs=[row, row, row],
        out_shape=[out, out, out],
        compiler_params=_params(("parallel",)),
        name="qkv_rope",
    )(x, g_pre, w_qkv, cos_tab, sin_tab)


def _attn_kernel(lq1_ref, lk1_ref, lq2_ref, lk2_ref, gsub_ref, q_ref, k_ref, v_ref, o_ref, v_ones, *,
                 lambda_init, q_tile):
    lam = (jnp.exp(jnp.sum(lq1_ref[...] * lk1_ref[...], axis=-1, keepdims=True))
           - jnp.exp(jnp.sum(lq2_ref[...] * lk2_ref[...], axis=-1, keepdims=True))
           + lambda_init)
    k = k_ref[...]
    v_ones[:, :LANES] = v_ref[...]
    v_ones[:, LANES:] = jnp.ones(v_ref.shape, BF16)
    v1 = v_ones[...]
    lane = lax.broadcasted_iota(jnp.int32, (1, LANES), 1)

    def softmax_pv(qm):
        s = lax.dot_general(qm, k, (((1,), (1,)), ((), ())), preferred_element_type=F32)
        e = jnp.exp2(s - jnp.max(s, axis=-1, keepdims=True))
        ol = _dot(e.astype(BF16), v1)
        return ol[:, :LANES] / ol[:, LANES:LANES + 1]

    for t in range(q_ref.shape[0] // q_tile):
        q = q_ref[t * q_tile:(t + 1) * q_tile, :]
        zero = jnp.zeros_like(q)
        o1 = softmax_pv(jnp.where(lane < DIFF_HEAD_DIM, q, zero))
        o2 = softmax_pv(jnp.where(lane >= DIFF_HEAD_DIM, q, zero))
        o = o1 - lam * o2
        o = _rms(o, gsub_ref[...], LN_EPS) * (1.0 - lambda_init)
        o_ref[t * q_tile:(t + 1) * q_tile, :] = o.astype(BF16)


def _attention(q, k, v, lam_q1, lam_k1, lam_q2, lam_k2, g_subln, lambda_init):
    b, s, d = q.shape
    heads = d // LANES
    seq_spec = pl.BlockSpec((None, s, LANES), lambda bi, hi: (bi, 0, hi))
    return pl.pallas_call(
        functools.partial(_attn_kernel, lambda_init=lambda_init, q_tile=Q_TILE),
        grid=(b, heads),
        in_specs=[_const_spec((1, DIFF_HEAD_DIM))] * 4 + [_const_spec((1, LANES)), seq_spec, seq_spec, seq_spec],
        out_specs=seq_spec,
        out_shape=jax.ShapeDtypeStruct((b, s, d), BF16),
        scratch_shapes=[pltpu.VMEM((s, 2 * LANES), BF16)],
        compiler_params=_params(("parallel", "parallel")),
        name="diff_attn",
    )(lam_q1, lam_k1, lam_q2, lam_k2, g_subln, q, k, v)


def _conv_mixer_tile(refs, scratch, tile, *, tiles_per_seq):
    (xp_ref, x_ref, xn_ref, gpre_ref, w1_ref, b1_ref, wdw_ref, bdw_ref, gcln_ref, bcln_ref,
     w2_ref, b2_ref, gpost_ref) = refs
    (zbuf,) = scratch
    ts, d = x_ref.shape
    at_seq_start = (tile % tiles_per_seq) == 0
    at_seq_end = (tile % tiles_per_seq) == tiles_per_seq - 1

    x = x_ref[...]
    h = _rms(jnp.concatenate([xp_ref[...], x, xn_ref[...]], axis=0), gpre_ref[...], NORM_EPS).astype(BF16)
    a = _dot(h, w1_ref[:, :d]) + b1_ref[:, :d]
    dep = yield a[ts + 2 * HALO - SUBLANES:, :]
    z = a * _sigmoid(_dot(h, w1_ref[:, d:]) + _gated(b1_ref[:, d:], dep))
    zbuf[HALO:HALO + ts, :] = z[HALO:HALO + ts, :]
    zbuf[0:HALO, :] = jnp.where(at_seq_start, 0.0, z[0:HALO, :])
    zbuf[HALO + ts:, :] = jnp.where(at_seq_end, 0.0, z[HALO + ts:, :])
    dep = yield z[ts + 2 * HALO - SUBLANES:, :]

    acc = None
    for phase in range(SUBLANES):
        part = None
        for j in range(CONV_WIDTH):
            off = HALO - CONV_PAD + j
            if off % SUBLANES != phase:
                continue
            base = off - phase
            term = _gated(wdw_ref[j:j + 1, :], dep) * zbuf[base:base + ts + SUBLANES, :]
            part = term if part is None else part + term
        if part is None:
            continue
        part = part[phase:phase + ts, :]
        acc = part if acc is None else acc + part
        dep = yield acc[ts - SUBLANES:, :]
    acc = acc + _gated(bdw_ref[...], dep)

    y = _layernorm(acc, gcln_ref[...], bcln_ref[...], LN_EPS)
    y = y * _sigmoid(y)
    dep = yield y[ts - SUBLANES:, :]
    m = _dot(y.astype(BF16), w2_ref[...]) + _gated(b2_ref[...], dep)
    return x + _rms(m, gpost_ref[...], NORM_EPS)


def _conv_ffn_layer(x, seq, g_pre, w_pw1, b_pw1, w_dw, b_dw, g_cln, b_cln, w_pw2, b_pw2, g_post, ffn_params):
    t, d = x.shape
    ts = TOKEN_TILE
    n_tiles = t // ts
    halo_per_tile = ts // HALO
    n_halo_blocks = t // HALO

    def tile(s):
        return jnp.minimum(s, n_tiles - 1)

    row = pl.BlockSpec((ts, d), lambda s: (tile(s), 0))
    prev = pl.BlockSpec((HALO, d), lambda s: (jnp.maximum(tile(s) * halo_per_tile - 1, 0), 0))
    nxt = pl.BlockSpec((HALO, d), lambda s: (jnp.minimum((tile(s) + 1) * halo_per_tile, n_halo_blocks - 1), 0))
    specs = [prev, row, nxt, _const_spec((1, d)), _const_spec((d, 2 * d)), _const_spec((1, 2 * d)),
             _const_spec((CONV_WIDTH, d)), _const_spec((1, d)), _const_spec((1, d)), _const_spec((1, d)),
             _const_spec((d, d)), _const_spec((1, d)), _const_spec((1, d))]
    args = (x, x, x, g_pre, w_pw1, b_pw1, w_dw, b_dw, g_cln, b_cln, w_pw2, b_pw2, g_post)
    return _mixer_ffn(functools.partial(_conv_mixer_tile, tiles_per_seq=seq // ts), args, specs,
                      [pltpu.VMEM((ts + 2 * HALO, d), F32)], t, d, ffn_params, "conformer_conv_ffn", lockstep=True)


def _sg_mixer_tile(refs, scratch, tile):
    del tile
    x_ref, gpre_ref, wuv_ref, buv_ref, gsln_ref, bsln_ref, ws_ref, bs_ref, wo_ref, bo_ref, gpost_ref = refs
    (gated,) = scratch
    tm, d = x_ref.shape
    n_chunks = tm // CHUNK
    x = x_ref[...]
    h = _rms(x, gpre_ref[...], NORM_EPS).astype(BF16)
    u = jax.nn.gelu(_dot(h, wuv_ref[:, :d]) + buv_ref[:, :d])
    yield
    v = jax.nn.gelu(_dot(h, wuv_ref[:, d:]) + buv_ref[:, d:])
    v = _layernorm(v, gsln_ref[...], bsln_ref[...], LN_EPS).astype(BF16)
    for g in range(SG_GROUPS):
        yield
        cl = g * LANES
        b_g = bs_ref[g]
        v_g = jnp.concatenate([v[c * CHUNK:(c + 1) * CHUNK, cl:cl + LANES] for c in range(n_chunks)], axis=1)
        mixed = _dot(ws_ref[g], v_g)
        for c in range(n_chunks):
            rl = c * CHUNK
            m_c = mixed[:, c * LANES:(c + 1) * LANES] + b_g
            gated[rl:rl + CHUNK, cl:cl + LANES] = (u[rl:rl + CHUNK, cl:cl + LANES] * m_c).astype(BF16)
    yield
    m = _dot(gated[...], wo_ref[...]) + bo_ref[...]
    return x + _rms(m, gpost_ref[...], NORM_EPS)


def _sg_ffn_layer(x, g_pre, w_uv, b_uv, g_sln, b_sln, w_s, b_s_bcast, w_o, b_o, g_post, ffn_params):
    t, d = x.shape
    tm = TOKEN_TILE
    n_tiles = t // tm
    row = pl.BlockSpec((tm, d), lambda s: (jnp.minimum(s, n_tiles - 1), 0))
    specs = [row, _const_spec((1, d)), _const_spec((d, 2 * d)), _const_spec((1, 2 * d)),
             _const_spec((1, d)), _const_spec((1, d)), _const_spec((SG_GROUPS, CHUNK, CHUNK)),
             _const_spec((SG_GROUPS, CHUNK, LANES)), _const_spec((d, d)), _const_spec((1, d)),
             _const_spec((1, d))]
    args = (x, g_pre, w_uv, b_uv, g_sln, b_sln, w_s, b_s_bcast, w_o, b_o, g_post)
    return _mixer_ffn(_sg_mixer_tile, args, specs, [pltpu.VMEM((tm, d), BF16)], t, d, ffn_params,
                      "spatial_gating_ffn", lockstep=False)


def _rope_tables(seq):
    pos = jnp.arange(seq, dtype=F32)
    inv_freq = 1.0 / (ROPE_THETA ** (jnp.arange(0, DIFF_HEAD_DIM, 2, dtype=F32) / DIFF_HEAD_DIM))
    ang = pos[:, None] * inv_freq[None, :]
    cos, sin = jnp.cos(ang), jnp.sin(ang)
    reps = LANES // DIFF_HEAD_DIM
    cos_tab = jnp.tile(jnp.concatenate([cos, cos], axis=-1), (1, reps))
    sin_tab = jnp.tile(jnp.concatenate([-sin, sin], axis=-1), (1, reps))
    return cos_tab, sin_tab


def _row(v):
    return v.reshape(1, -1)


def kernel(x_prompt, x_sample, l0_g_mix_pre, l0_g_mix_post, l0_w_qkv, l0_lam_q1, l0_lam_k1, l0_lam_q2, l0_lam_k2, l0_g_subln, l0_w_o, l0_g_ffn_pre, l0_g_ffn_post, l0_w_gate_up, l0_w_down, l1_g_mix_pre, l1_g_mix_post, l1_w_pw1, l1_b_pw1, l1_w_dw, l1_b_dw, l1_g_cln, l1_b_cln, l1_w_pw2, l1_b_pw2, l1_g_ffn_pre, l1_g_ffn_post, l1_w_gate_up, l1_w_down, l2_g_mix_pre, l2_g_mix_post, l2_w_uv, l2_b_uv, l2_g_sln, l2_b_sln, l2_w_s, l2_b_s, l2_w_o, l2_b_o, l2_g_ffn_pre, l2_g_ffn_post, l2_w_gate_up, l2_w_down, l3_g_mix_pre, l3_g_mix_post, l3_w_qkv, l3_lam_q1, l3_lam_k1, l3_lam_q2, l3_lam_k2, l3_g_subln, l3_w_o, l3_g_ffn_pre, l3_g_ffn_post, l3_w_gate_up, l3_w_down):
    p = dict(locals())
    depth = 4
    seq = x_prompt.shape[1]
    cos_tab, sin_tab = _rope_tables(seq)

    def w16(name):
        return p[name].astype(BF16)

    layers = []
    for i in range(depth):
        n = f"l{i}_"
        kind = i % N_MIXERS
        lp = {"kind": kind,
              "g_mix_pre": _row(p[n + "g_mix_pre"]), "g_mix_post": _row(p[n + "g_mix_post"]),
              "g_ffn_pre": _row(p[n + "g_ffn_pre"]), "g_ffn_post": _row(p[n + "g_ffn_post"]),
              "w_gate_up": w16(n + "w_gate_up"), "w_down": w16(n + "w_down")}
        if kind == 0:
            lp.update(w_qkv=w16(n + "w_qkv"), w_o=w16(n + "w_o"), g_subln=_row(p[n + "g_subln"]),
                      lam=[_row(p[n + k]) for k in ("lam_q1", "lam_k1", "lam_q2", "lam_k2")],
                      lambda_init=_lambda_init(i))
        elif kind == 1:
            lp.update(w_pw1=w16(n + "w_pw1"), b_pw1=_row(p[n + "b_pw1"]), w_dw=p[n + "w_dw"],
                      b_dw=_row(p[n + "b_dw"]), g_cln=_row(p[n + "g_cln"]), b_cln=_row(p[n + "b_cln"]),
                      w_pw2=w16(n + "w_pw2"), b_pw2=_row(p[n + "b_pw2"]))
        else:
            b_s = p[n + "b_s"]
            lp.update(w_uv=w16(n + "w_uv"), b_uv=_row(p[n + "b_uv"]), g_sln=_row(p[n + "g_sln"]),
                      b_sln=_row(p[n + "b_sln"]), w_s=w16(n + "w_s"),
                      b_s=jnp.broadcast_to(b_s[:, :, None], b_s.shape + (LANES,)),
                      w_o=w16(n + "w_o"), b_o=_row(p[n + "b_o"]))
        layers.append(lp)

    def trunk(x3):
        b, s, d = x3.shape
        x = x3.reshape(b * s, d)
        for lp in layers:
            ffn_params = (lp["g_ffn_pre"], lp["g_ffn_post"], lp["w_gate_up"], lp["w_down"])
            if lp["kind"] == 0:
                q, k, v = _qkv(x, lp["g_mix_pre"], lp["w_qkv"], cos_tab, sin_tab, s)
                a = _attention(q.reshape(b, s, d), k.reshape(b, s, d), v.reshape(b, s, d),
                               *lp["lam"], lp["g_subln"], lp["lambda_init"])
                x = _proj_ffn(a.reshape(b * s, d), lp["w_o"], lp["g_mix_post"], x, ffn_params)
            elif lp["kind"] == 1:
                x = _conv_ffn_layer(x, s, lp["g_mix_pre"], lp["w_pw1"], lp["b_pw1"], lp["w_dw"], lp["b_dw"],
                                    lp["g_cln"], lp["b_cln"], lp["w_pw2"], lp["b_pw2"], lp["g_mix_post"],
                                    ffn_params)
            else:
                x = _sg_ffn_layer(x, lp["g_mix_pre"], lp["w_uv"], lp["b_uv"], lp["g_sln"], lp["b_sln"],
                                  lp["w_s"], lp["b_s"], lp["w_o"], lp["b_o"], lp["g_mix_post"], ffn_params)
        return x.reshape(b, s, d)

    return (trunk(x_prompt), trunk(x_sample))
```

```python
import functools
import math

import jax
import jax.numpy as jnp
from jax import lax
from jax.experimental import pallas as pl
from jax.experimental.pallas import tpu as pltpu

F32 = jnp.float32
BF16 = jnp.bfloat16

N_MIXERS = 3
DIFF_HEAD_DIM = 64
ROPE_THETA = 10000.0
CONV_WIDTH = 31
CONV_PAD = (CONV_WIDTH - 1) // 2
CHUNK = 128
SG_GROUPS = 8
NORM_EPS = 1e-6
LN_EPS = 1e-5
LOG2_E = 1.4426950408889634

LANES = 128
SUBLANES = 8
HALO = 16
VMEM_LIMIT = 56 * 1024 * 1024

TOKEN_TILE = 512
DENSE_TILE = 1024
Q_TILE = 256
FF_CHUNK = 256


def _lambda_init(layer_idx):
    return 0.8 - 0.6 * math.exp(-0.3 * layer_idx)


def _rms(x, g, eps):
    return x * lax.rsqrt(jnp.mean(x * x, axis=-1, keepdims=True) + eps) * g


def _layernorm(x, g, b, eps):
    mu = jnp.mean(x, axis=-1, keepdims=True)
    xc = x - mu
    var = jnp.mean(xc * xc, axis=-1, keepdims=True)
    return xc * lax.rsqrt(var + eps) * g + b


def _sigmoid(x):
    return 1.0 / (1.0 + jnp.exp(-x))


def _dot(a, b):
    return jnp.dot(a, b, preferred_element_type=F32)


def _const_spec(shape):
    nd = len(shape)
    return pl.BlockSpec(shape, lambda *_: (0,) * nd, pipeline_mode=pl.Buffered(1))


def _params(dimension_semantics):
    return pltpu.CompilerParams(dimension_semantics=dimension_semantics, vmem_limit_bytes=VMEM_LIMIT)


def _zero_after(dep):
    bits = lax.bitcast_convert_type(dep, jnp.uint32)
    bits = lax.shift_right_logical(lax.shift_right_logical(bits, jnp.uint32(16)), jnp.uint32(16))
    return lax.bitcast_convert_type(bits, F32)[0:1, :]


def _gated(row, dep):
    return row if dep is None else row + _zero_after(dep)


def _run_stages(*stage_gens, lockstep=False):
    n = len(stage_gens)
    assert not lockstep or n == 2
    pieces = [None] * n
    results = [None] * n
    started = [False] * n
    live = list(range(n))
    while live:
        previous = list(pieces)
        for i in list(live):
            dep = previous[1 - i] if lockstep else None
            try:
                pieces[i] = stage_gens[i].send(dep) if started[i] else next(stage_gens[i])
                started[i] = True
            except StopIteration as done:
                results[i] = done.value
                live.remove(i)
    return results


def _ffn_stages(x, gpre_ref, gpost_ref, wgu_ref, wd_ref):
    d_ff = wd_ref.shape[0]
    tm = x.shape[0]
    h = _rms(x, gpre_ref[...], NORM_EPS).astype(BF16)
    acc = None
    dep = yield h[tm - 2 * SUBLANES:, :].astype(F32)[SUBLANES:, :]
    for c in range(d_ff // FF_CHUNK):
        lo = c * FF_CHUNK
        wg = wgu_ref[:, lo:lo + FF_CHUNK]
        wu = wgu_ref[:, d_ff + lo:d_ff + lo + FF_CHUNK]
        if dep is not None:
            zero = _zero_after(dep)[:, :FF_CHUNK].astype(BF16)
            wg = wg + zero
            wu = wu + zero
        g = _dot(h, wg)
        u = _dot(h, wu)
        a = (g * _sigmoid(g) * u).astype(BF16)
        part = _dot(a, wd_ref[lo:lo + FF_CHUNK, :])
        acc = part if acc is None else acc + part
        dep = yield part[tm - SUBLANES:, :]
    g_post = gpost_ref[...]
    if dep is not None:
        g_post = g_post + _zero_after(dep)
    return x + _rms(acc, g_post, NORM_EPS)


def _ffn_specs(d, d_ff):
    return [_const_spec((1, d)), _const_spec((1, d)), _const_spec((d, 2 * d_ff)), _const_spec((d_ff, d))]


def _proj_ffn_stages(a_ref, wo_ref, gmix_ref, x_ref, ffn_refs, rows):
    a = jnp.concatenate([a_ref[h, rows, :] for h in range(a_ref.shape[0])], axis=1)
    x = x_ref[rows, :] + _rms(_dot(a, wo_ref[...]), gmix_ref[...], NORM_EPS)
    return (yield from _ffn_stages(x, *ffn_refs))


def _proj_ffn_kernel(a_ref, wo_ref, gmix_ref, x_ref, gpre_ref, gpost_ref, wgu_ref, wd_ref, o_ref):
    half = x_ref.shape[0] // 2
    ffn_refs = (gpre_ref, gpost_ref, wgu_ref, wd_ref)
    o_ref[:half, :], o_ref[half:, :] = _run_stages(
        _proj_ffn_stages(a_ref, wo_ref, gmix_ref, x_ref, ffn_refs, slice(0, half)),
        _proj_ffn_stages(a_ref, wo_ref, gmix_ref, x_ref, ffn_refs, slice(half, 2 * half)))


def _proj_ffn(a, w_o, g_mix_post, x, ffn_params):
    t, d = x.shape
    d_ff = ffn_params[3].shape[0]
    tm = DENSE_TILE
    row = pl.BlockSpec((tm, d), lambda i: (i, 0))
    _, heads, seq, _ = a.shape
    tiles_per_seq = seq // tm
    head_rows = pl.BlockSpec((None, heads, tm, LANES), lambda i: (i // tiles_per_seq, 0, i % tiles_per_seq, 0))
    return pl.pallas_call(
        _proj_ffn_kernel,
        grid=(t // tm,),
        in_specs=[head_rows, _const_spec((d, d)), _const_spec((1, d)), row] + _ffn_specs(d, d_ff),
        out_specs=row,
        out_shape=jax.ShapeDtypeStruct((t, d), F32),
        compiler_params=_params(("parallel",)),
        name="attn_proj_ffn",
    )(a, w_o, g_mix_post, x, *ffn_params)


def _mixer_ffn_kernel(*refs, mixer_fn, n_mixer_refs, n_tiles, lockstep):
    mixer_refs = refs[:n_mixer_refs]
    gpre_ref, gpost_ref, wgu_ref, wd_ref, o_ref, ybuf = refs[n_mixer_refs:n_mixer_refs + 6]
    mixer_scratch = refs[n_mixer_refs + 6:]
    s = pl.program_id(0)

    @pl.when(s == 0)
    def _():
        ybuf[...] = jnp.zeros_like(ybuf)

    o_ref[...], ybuf[...] = _run_stages(
        _ffn_stages(ybuf[...], gpre_ref, gpost_ref, wgu_ref, wd_ref),
        mixer_fn(mixer_refs, mixer_scratch, jnp.minimum(s, n_tiles - 1)), lockstep=lockstep)


def _mixer_ffn(mixer_fn, mixer_args, mixer_specs, mixer_scratch, t, d, ffn_params, name, lockstep):
    d_ff = ffn_params[3].shape[0]
    tm = TOKEN_TILE
    n_tiles = t // tm
    return pl.pallas_call(
        functools.partial(_mixer_ffn_kernel, mixer_fn=mixer_fn, n_mixer_refs=len(mixer_args), n_tiles=n_tiles,
                          lockstep=lockstep),
        grid=(n_tiles + 1,),
        in_specs=list(mixer_specs) + _ffn_specs(d, d_ff),
        out_specs=pl.BlockSpec((tm, d), lambda s: (jnp.maximum(s - 1, 0), 0)),
        out_shape=jax.ShapeDtypeStruct((t, d), F32),
        scratch_shapes=[pltpu.VMEM((tm, d), F32)] + list(mixer_scratch),
        compiler_params=_params(("arbitrary",)),
        name=name,
    )(*mixer_args, *ffn_params)


def _rope_block(xb, cos, sin_signed, first_half):
    x_up = pltpu.roll(xb, LANES - DIFF_HEAD_DIM // 2, 1)
    x_dn = pltpu.roll(xb, DIFF_HEAD_DIM // 2, 1)
    return xb * cos + jnp.where(first_half, x_up, x_dn) * sin_signed


def _qkv_kernel(x_ref, g_ref, w_ref, cos_ref, sin_ref, q_ref, k_ref, v_ref):
    d = x_ref.shape[1]
    h = _rms(x_ref[...], g_ref[...], NORM_EPS).astype(BF16)
    cos = cos_ref[...]
    sin_signed = sin_ref[...]
    lane = lax.broadcasted_iota(jnp.int32, (1, LANES), 1)
    first_half = (lane % DIFF_HEAD_DIM) < (DIFF_HEAD_DIM // 2)
    q_scale = DIFF_HEAD_DIM ** -0.5 * LOG2_E
    q = _dot(h, w_ref[:, :d])
    for hb in range(d // LANES):
        lo = hb * LANES
        q_ref[hb] = (_rope_block(q[:, lo:lo + LANES], cos, sin_signed, first_half) * q_scale).astype(BF16)
    k = _dot(h, w_ref[:, d:2 * d])
    for hb in range(d // LANES):
        lo = hb * LANES
        k_ref[hb] = _rope_block(k[:, lo:lo + LANES], cos, sin_signed, first_half).astype(BF16)
    v = _dot(h, w_ref[:, 2 * d:]).astype(BF16)
    for hb in range(d // LANES):
        v_ref[hb] = v[:, hb * LANES:(hb + 1) * LANES]


def _qkv(x, g_pre, w_qkv, cos_tab, sin_tab, seq):
    t, d = x.shape
    tm = DENSE_TILE
    row = pl.BlockSpec((tm, d), lambda i: (i, 0))
    tiles_per_seq = seq // tm
    tab = pl.BlockSpec((tm, LANES), lambda i: (i % tiles_per_seq, 0))
    heads = d // LANES
    out = jax.ShapeDtypeStruct((t // seq, heads, seq, LANES), BF16)
    head_rows = pl.BlockSpec((None, heads, tm, LANES), lambda i: (i // tiles_per_seq, 0, i % tiles_per_seq, 0))
    return pl.pallas_call(
        _qkv_kernel,
        grid=(t // tm,),
        in_specs=[row, _const_spec((1, d)), _const_spec((d, 3 * d)), tab, tab],
        out_specs=[head_rows, head_rows, head_rows],
        out_shape=[out, out, out],
        compiler_params=_params(("parallel",)),
        name="qkv_rope",
    )(x, g_pre, w_qkv, cos_tab, sin_tab)


def _attn_kernel(lq1_ref, lk1_ref, lq2_ref, lk2_ref, gsub_ref, q_ref, k_ref, v_ref, o_ref, v_ones, *,
                 lambda_init, q_tile):
    lam = (jnp.exp(jnp.sum(lq1_ref[...] * lk1_ref[...], axis=-1, keepdims=True))
           - jnp.exp(jnp.sum(lq2_ref[...] * lk2_ref[...], axis=-1, keepdims=True))
           + lambda_init)
    k = k_ref[...]
    v_ones[:, :LANES] = v_ref[...]
    v_ones[:, LANES:] = jnp.ones(v_ref.shape, BF16)
    v1 = v_ones[...]
    lane = lax.broadcasted_iota(jnp.int32, (1, LANES), 1)

    def softmax_pv(qm):
        s = lax.dot_general(qm, k, (((1,), (1,)), ((), ())), preferred_element_type=F32)
        e = jnp.exp2(s - jnp.max(s, axis=-1, keepdims=True))
        ol = _dot(e.astype(BF16), v1)
        return ol[:, :LANES] / ol[:, LANES:LANES + 1]

    for t in range(q_ref.shape[0] // q_tile):
        q = q_ref[t * q_tile:(t + 1) * q_tile, :]
        zero = jnp.zeros_like(q)
        o1 = softmax_pv(jnp.where(lane < DIFF_HEAD_DIM, q, zero))
        o2 = softmax_pv(jnp.where(lane >= DIFF_HEAD_DIM, q, zero))
        o = o1 - lam * o2
        o = _rms(o, gsub_ref[...], LN_EPS) * (1.0 - lambda_init)
        o_ref[t * q_tile:(t + 1) * q_tile, :] = o.astype(BF16)


def _attention(q, k, v, lam_q1, lam_k1, lam_q2, lam_k2, g_subln, lambda_init):
    b, heads, s, _ = q.shape
    seq_spec = pl.BlockSpec((None, None, s, LANES), lambda bi, hi: (bi, hi, 0, 0))
    return pl.pallas_call(
        functools.partial(_attn_kernel, lambda_init=lambda_init, q_tile=Q_TILE),
        grid=(b, heads),
        in_specs=[_const_spec((1, DIFF_HEAD_DIM))] * 4 + [_const_spec((1, LANES)), seq_spec, seq_spec, seq_spec],
        out_specs=seq_spec,
        out_shape=jax.ShapeDtypeStruct((b, heads, s, LANES), BF16),
        scratch_shapes=[pltpu.VMEM((s, 2 * LANES), BF16)],
        compiler_params=_params(("parallel", "parallel")),
        name="diff_attn",
    )(lam_q1, lam_k1, lam_q2, lam_k2, g_subln, q, k, v)


def _conv_mixer_tile(refs, scratch, tile, *, tiles_per_seq):
    (xp_ref, x_ref, xn_ref, gpre_ref, w1_ref, b1_ref, wdw_ref, bdw_ref, gcln_ref, bcln_ref,
     w2_ref, b2_ref, gpost_ref) = refs
    (zbuf,) = scratch
    ts, d = x_ref.shape
    at_seq_start = (tile % tiles_per_seq) == 0
    at_seq_end = (tile % tiles_per_seq) == tiles_per_seq - 1

    x = x_ref[...]
    h = _rms(jnp.concatenate([xp_ref[...], x, xn_ref[...]], axis=0), gpre_ref[...], NORM_EPS).astype(BF16)
    a = _dot(h, w1_ref[:, :d]) + b1_ref[:, :d]
    dep = yield a[ts + 2 * HALO - SUBLANES:, :]
    z = a * _sigmoid(_dot(h, w1_ref[:, d:]) + _gated(b1_ref[:, d:], dep))
    zbuf[HALO:HALO + ts, :] = z[HALO:HALO + ts, :]
    zbuf[0:HALO, :] = jnp.where(at_seq_start, 0.0, z[0:HALO, :])
    zbuf[HALO + ts:, :] = jnp.where(at_seq_end, 0.0, z[HALO + ts:, :])
    dep = yield z[ts + 2 * HALO - SUBLANES:, :]

    acc = None
    for phase in range(SUBLANES):
        part = None
        for j in range(CONV_WIDTH):
            off = HALO - CONV_PAD + j
            if off % SUBLANES != phase:
                continue
            base = off - phase
            term = _gated(wdw_ref[j:j + 1, :], dep) * zbuf[base:base + ts + SUBLANES, :]
            part = term if part is None else part + term
        if part is None:
            continue
        part = part[phase:phase + ts, :]
        acc = part if acc is None else acc + part
        dep = yield acc[ts - SUBLANES:, :]
    acc = acc + _gated(bdw_ref[...], dep)

    y = _layernorm(acc, gcln_ref[...], bcln_ref[...], LN_EPS)
    y = y * _sigmoid(y)
    dep = yield y[ts - SUBLANES:, :]
    m = _dot(y.astype(BF16), w2_ref[...]) + _gated(b2_ref[...], dep)
    return x + _rms(m, gpost_ref[...], NORM_EPS)


def _conv_ffn_layer(x, seq, g_pre, w_pw1, b_pw1, w_dw, b_dw, g_cln, b_cln, w_pw2, b_pw2, g_post, ffn_params):
    t, d = x.shape
    ts = TOKEN_TILE
    n_tiles = t // ts
    halo_per_tile = ts // HALO
    n_halo_blocks = t // HALO

    def tile(s):
        return jnp.minimum(s, n_tiles - 1)

    row = pl.BlockSpec((ts, d), lambda s: (tile(s), 0))
    prev = pl.BlockSpec((HALO, d), lambda s: (jnp.maximum(tile(s) * halo_per_tile - 1, 0), 0))
    nxt = pl.BlockSpec((HALO, d), lambda s: (jnp.minimum((tile(s) + 1) * halo_per_tile, n_halo_blocks - 1), 0))
    specs = [prev, row, nxt, _const_spec((1, d)), _const_spec((d, 2 * d)), _const_spec((1, 2 * d)),
             _const_spec((CONV_WIDTH, d)), _const_spec((1, d)), _const_spec((1, d)), _const_spec((1, d)),
             _const_spec((d, d)), _const_spec((1, d)), _const_spec((1, d))]
    args = (x, x, x, g_pre, w_pw1, b_pw1, w_dw, b_dw, g_cln, b_cln, w_pw2, b_pw2, g_post)
    return _mixer_ffn(functools.partial(_conv_mixer_tile, tiles_per_seq=seq // ts), args, specs,
                      [pltpu.VMEM((ts + 2 * HALO, d), F32)], t, d, ffn_params, "conformer_conv_ffn", lockstep=True)


def _sg_mixer_tile(refs, scratch, tile):
    del tile
    x_ref, gpre_ref, wuv_ref, buv_ref, gsln_ref, bsln_ref, ws_ref, bs_ref, wo_ref, bo_ref, gpost_ref = refs
    (gated,) = scratch
    tm, d = x_ref.shape
    n_chunks = tm // CHUNK
    x = x_ref[...]
    h = _rms(x, gpre_ref[...], NORM_EPS).astype(BF16)
    u = jax.nn.gelu(_dot(h, wuv_ref[:, :d]) + buv_ref[:, :d])
    yield
    v = jax.nn.gelu(_dot(h, wuv_ref[:, d:]) + buv_ref[:, d:])
    v = _layernorm(v, gsln_ref[...], bsln_ref[...], LN_EPS).astype(BF16)
    for g in range(SG_GROUPS):
        yield
        cl = g * LANES
        b_g = bs_ref[g]
        v_g = jnp.concatenate([v[c * CHUNK:(c + 1) * CHUNK, cl:cl + LANES] for c in range(n_chunks)], axis=1)
        mixed = _dot(ws_ref[g], v_g)
        for c in range(n_chunks):
            rl = c * CHUNK
            m_c = mixed[:, c * LANES:(c + 1) * LANES] + b_g
            gated[rl:rl + CHUNK, cl:cl + LANES] = (u[rl:rl + CHUNK, cl:cl + LANES] * m_c).astype(BF16)
    yield
    m = _dot(gated[...], wo_ref[...]) + bo_ref[...]
    return x + _rms(m, gpost_ref[...], NORM_EPS)


def _sg_ffn_layer(x, g_pre, w_uv, b_uv, g_sln, b_sln, w_s, b_s_bcast, w_o, b_o, g_post, ffn_params):
    t, d = x.shape
    tm = TOKEN_TILE
    n_tiles = t // tm
    row = pl.BlockSpec((tm, d), lambda s: (jnp.minimum(s, n_tiles - 1), 0))
    specs = [row, _const_spec((1, d)), _const_spec((d, 2 * d)), _const_spec((1, 2 * d)),
             _const_spec((1, d)), _const_spec((1, d)), _const_spec((SG_GROUPS, CHUNK, CHUNK)),
             _const_spec((SG_GROUPS, CHUNK, LANES)), _const_spec((d, d)), _const_spec((1, d)),
             _const_spec((1, d))]
    args = (x, g_pre, w_uv, b_uv, g_sln, b_sln, w_s, b_s_bcast, w_o, b_o, g_post)
    return _mixer_ffn(_sg_mixer_tile, args, specs, [pltpu.VMEM((tm, d), BF16)], t, d, ffn_params,
                      "spatial_gating_ffn", lockstep=False)


def _rope_tables(seq):
    pos = jnp.arange(seq, dtype=F32)
    inv_freq = 1.0 / (ROPE_THETA ** (jnp.arange(0, DIFF_HEAD_DIM, 2, dtype=F32) / DIFF_HEAD_DIM))
    ang = pos[:, None] * inv_freq[None, :]
    cos, sin = jnp.cos(ang), jnp.sin(ang)
    reps = LANES // DIFF_HEAD_DIM
    cos_tab = jnp.tile(jnp.concatenate([cos, cos], axis=-1), (1, reps))
    sin_tab = jnp.tile(jnp.concatenate([-sin, sin], axis=-1), (1, reps))
    return cos_tab, sin_tab


def _row(v):
    return v.reshape(1, -1)


def kernel(x_prompt, x_sample, l0_g_mix_pre, l0_g_mix_post, l0_w_qkv, l0_lam_q1, l0_lam_k1, l0_lam_q2, l0_lam_k2, l0_g_subln, l0_w_o, l0_g_ffn_pre, l0_g_ffn_post, l0_w_gate_up, l0_w_down, l1_g_mix_pre, l1_g_mix_post, l1_w_pw1, l1_b_pw1, l1_w_dw, l1_b_dw, l1_g_cln, l1_b_cln, l1_w_pw2, l1_b_pw2, l1_g_ffn_pre, l1_g_ffn_post, l1_w_gate_up, l1_w_down, l2_g_mix_pre, l2_g_mix_post, l2_w_uv, l2_b_uv, l2_g_sln, l2_b_sln, l2_w_s, l2_b_s, l2_w_o, l2_b_o, l2_g_ffn_pre, l2_g_ffn_post, l2_w_gate_up, l2_w_down, l3_g_mix_pre, l3_g_mix_post, l3_w_qkv, l3_lam_q1, l3_lam_k1, l3_lam_q2, l3_lam_k2, l3_g_subln, l3_w_o, l3_g_ffn_pre, l3_g_ffn_post, l3_w_gate_up, l3_w_down):
    p = dict(locals())
    depth = 4
    seq = x_prompt.shape[1]
    cos_tab, sin_tab = _rope_tables(seq)

    def w16(name):
        return p[name].astype(BF16)

    layers = []
    for i in range(depth):
        n = f"l{i}_"
        kind = i % N_MIXERS
        lp = {"kind": kind,
              "g_mix_pre": _row(p[n + "g_mix_pre"]), "g_mix_post": _row(p[n + "g_mix_post"]),
              "g_ffn_pre": _row(p[n + "g_ffn_pre"]), "g_ffn_post": _row(p[n + "g_ffn_post"]),
              "w_gate_up": w16(n + "w_gate_up"), "w_down": w16(n + "w_down")}
        if kind == 0:
            lp.update(w_qkv=w16(n + "w_qkv"), w_o=w16(n + "w_o"), g_subln=_row(p[n + "g_subln"]),
                      lam=[_row(p[n + k]) for k in ("lam_q1", "lam_k1", "lam_q2", "lam_k2")],
                      lambda_init=_lambda_init(i))
        elif kind == 1:
            lp.update(w_pw1=w16(n + "w_pw1"), b_pw1=_row(p[n + "b_pw1"]), w_dw=p[n + "w_dw"],
                      b_dw=_row(p[n + "b_dw"]), g_cln=_row(p[n + "g_cln"]), b_cln=_row(p[n + "b_cln"]),
                      w_pw2=w16(n + "w_pw2"), b_pw2=_row(p[n + "b_pw2"]))
        else:
            b_s = p[n + "b_s"]
            lp.update(w_uv=w16(n + "w_uv"), b_uv=_row(p[n + "b_uv"]), g_sln=_row(p[n + "g_sln"]),
                      b_sln=_row(p[n + "b_sln"]), w_s=w16(n + "w_s"),
                      b_s=jnp.broadcast_to(b_s[:, :, None], b_s.shape + (LANES,)),
                      w_o=w16(n + "w_o"), b_o=_row(p[n + "b_o"]))
        layers.append(lp)

    def trunk(x3):
        b, s, d = x3.shape
        x = x3.reshape(b * s, d)
        for lp in layers:
            ffn_params = (lp["g_ffn_pre"], lp["g_ffn_post"], lp["w_gate_up"], lp["w_down"])
            if lp["kind"] == 0:
                q, k, v = _qkv(x, lp["g_mix_pre"], lp["w_qkv"], cos_tab, sin_tab, s)
                a = _attention(q, k, v, *lp["lam"], lp["g_subln"], lp["lambda_init"])
                x = _proj_ffn(a, lp["w_o"], lp["g_mix_post"], x, ffn_params)
            elif lp["kind"] == 1:
                x = _conv_ffn_layer(x, s, lp["g_mix_pre"], lp["w_pw1"], lp["b_pw1"], lp["w_dw"], lp["b_dw"],
                                    lp["g_cln"], lp["b_cln"], lp["w_pw2"], lp["b_pw2"], lp["g_mix_post"],
                                    ffn_params)
            else:
                x = _sg_ffn_layer(x, lp["g_mix_pre"], lp["w_uv"], lp["b_uv"], lp["g_sln"], lp["b_sln"],
                                  lp["w_s"], lp["b_s"], lp["w_o"], lp["b_o"], lp["g_mix_post"], ffn_params)
        return x.reshape(b, s, d)

    return (trunk(x_prompt), trunk(x_sample))
```
